```python
import math
import jax
import jax.numpy as jnp
from jax import lax
import numpy as np

D_MODEL = 2048
BATCH = 8
SEQ = 4096
DEPTH = 4

CHUNK = 64
N_MIXERS = 4
D_MIX = D_MODEL
D_GROUP = D_MIX // N_MIXERS

HG_DIM = 128
HG_HEADS = D_GROUP // HG_DIM

S5_CH = 16
S5_GROUPS = D_GROUP // S5_CH
S5_STATE = 64
S5_DT_MIN = 1e-3
S5_DT_MAX = 1e-1

SB_DIM = 128
SB_HEADS = D_GROUP // SB_DIM
Q_BLOCK = 128

RW_DIM = 64
RW_HEADS = D_GROUP // RW_DIM
RW_W_RANK = 64
RW_A_RANK = 64
RW_G_RANK = 128
RW_GN_EPS = 64e-5
RW_SIZES = (D_GROUP, D_GROUP, D_GROUP, RW_W_RANK, RW_A_RANK, RW_G_RANK)
RW_SPLITS = tuple(int(s) for s in np.cumsum(RW_SIZES)[:-1])
RW_WIDTH = sum(RW_SIZES)
RW_OFFSET = 8 * D_GROUP
N_IN = RW_OFFSET + RW_WIDTH

D_FF = 256 * math.ceil(8 * D_MODEL / 3 / 256)
DEEPNORM_ALPHA = (2 * DEPTH) ** 0.25
DEEPNORM_BETA = (8 * DEPTH) ** -0.25
LN_EPS = 1e-5
RMS_EPS = 1e-6

kernel_name = 'hybrid_stream_encoder_hgrn2_s5_sb_rwkv7'


def _layer_norm(x, g, b):
    xf = x.astype(jnp.float32)
    mu = jnp.mean(xf, -1, keepdims=True)
    var = jnp.mean(jnp.square(xf - mu), -1, keepdims=True)
    y = (xf - mu) * lax.rsqrt(var + LN_EPS)
    return (y * g.astype(jnp.float32) + b.astype(jnp.float32)).astype(x.dtype)


def _rms_norm(xf):
    return xf * lax.rsqrt(jnp.mean(xf * xf, -1, keepdims=True) + RMS_EPS)


def _token_shift(x, mu):
    prev = jnp.pad(x, ((0, 0), (1, 0), (0, 0)))[:, :-1]
    return x + mu * (prev - x)


def _hgrn2(q, f_logit, i, g, lb, norm_g):
    B, S, _ = q.shape
    f32 = jnp.float32
    n_chunks = S // CHUNK
    lbf = lb.astype(f32)
    log_f = jnp.logaddexp(jnp.log(jnp.maximum(lbf, 0.0)),
                          jnp.log1p(-lbf) + jax.nn.log_sigmoid(f_logit.astype(f32)))
    k = -jnp.expm1(log_f)
    qf = jax.nn.silu(q.astype(f32))
    v = i.astype(f32)

    def to_chunks(t):
        return t.reshape(B, n_chunks, CHUNK, HG_HEADS, HG_DIM).transpose(1, 0, 3, 2, 4)

    causal = jnp.tril(jnp.ones((CHUNK, CHUNK), bool))[:, :, None]

    def step(state, inp):
        qb, kb, vb, lfb = inp
        bcum = jnp.cumsum(lfb, axis=-2)
        diff = bcum[..., :, None, :] - bcum[..., None, :, :]
        decay = jnp.exp(jnp.where(causal, diff, -jnp.inf))
        scores = jnp.einsum('bhtk,bhsk,bhtsk->bhts', qb, kb, decay)
        o = (jnp.einsum('bhts,bhsv->bhtv', scores, vb)
             + jnp.einsum('bhtk,bhkv->bhtv', qb * jnp.exp(bcum), state))
        b_last = bcum[..., -1:, :]
        state = (jnp.exp(b_last[..., 0, :])[..., None] * state
                 + jnp.einsum('bhsk,bhsv->bhkv', kb * jnp.exp(b_last - bcum), vb))
        return state, o

    s0 = jnp.zeros((B, HG_HEADS, HG_DIM, HG_DIM), f32)
    _, o = lax.scan(step, s0, (to_chunks(qf), to_chunks(k), to_chunks(v), to_chunks(log_f)))
    o = o.transpose(1, 0, 3, 2, 4).reshape(B, S, HG_HEADS, HG_DIM)
    o = _rms_norm(o).reshape(B, S, D_GROUP) * norm_g.astype(f32)
    return (o * jax.nn.silu(g.astype(f32))).astype(q.dtype)


def _s5(u, a_re, a_im, log_dt, b_re, b_im, c_re, c_im, d_skip, glu_w, glu_b):
    B, S, _ = u.shape
    f32 = jnp.float32
    uf = u.astype(f32)
    ug = uf.reshape(B, S, S5_GROUPS, S5_CH)
    dt = jnp.exp(log_dt.astype(f32))[:, None]
    ar, ai = a_re.astype(f32), a_im.astype(f32)
    mag = jnp.exp(ar * dt)
    lr, li = mag * jnp.cos(ai * dt), mag * jnp.sin(ai * dt)
    den = ar * ar + ai * ai
    zr = ((lr - 1.0) * ar + li * ai) / den
    zi = (li * ar - (lr - 1.0) * ai) / den
    br, bi = b_re.astype(f32), b_im.astype(f32)
    bbar_r = zr[..., None] * br - zi[..., None] * bi
    bbar_i = zr[..., None] * bi + zi[..., None] * br
    xr = jnp.einsum('bsgc,gpc->bsgp', ug, bbar_r)
    xi = jnp.einsum('bsgc,gpc->bsgp', ug, bbar_i)
    lam_r = jnp.broadcast_to(lr[None, None], (1, S, S5_GROUPS, S5_STATE))
    lam_i = jnp.broadcast_to(li[None, None], (1, S, S5_GROUPS, S5_STATE))

    def combine(e1, e2):
        a1r, a1i, b1r, b1i = e1
        a2r, a2i, b2r, b2i = e2
        return (a2r * a1r - a2i * a1i, a2r * a1i + a2i * a1r,
                a2r * b1r - a2i * b1i + b2r, a2r * b1i + a2i * b1r + b2i)

    _, _, hr, hi = lax.associative_scan(combine, (lam_r, lam_i, xr, xi), axis=1)
    y = (jnp.einsum('bsgp,gcp->bsgc', hr, c_re.astype(f32))
         - jnp.einsum('bsgp,gcp->bsgc', hi, c_im.astype(f32)))
    y = y.reshape(B, S, D_GROUP) + d_skip.astype(f32) * uf
    y = jax.nn.gelu(y)
    y = y * jax.nn.sigmoid(y @ glu_w.astype(f32) + glu_b.astype(f32))
    return y.astype(u.dtype)


def _stick_breaking(q, k, v):
    B, S, _ = q.shape
    f32 = jnp.float32

    def heads(t):
        return t.astype(f32).reshape(B, S, SB_HEADS, SB_DIM).transpose(0, 2, 1, 3)

    qh, kh, vh = heads(q) * (SB_DIM ** -0.5), heads(k), heads(v)
    n_blk = S // Q_BLOCK
    q_blocks = qh.reshape(B, SB_HEADS, n_blk, Q_BLOCK, SB_DIM).transpose(2, 0, 1, 3, 4)
    key_pos = jnp.arange(S)

    def block(args):
        qb, start = args
        z = jnp.einsum('bhtd,bhsd->bhts', qb, kh)
        q_pos = start + jnp.arange(Q_BLOCK)
        mask = key_pos[None, :] < q_pos[:, None]
        log_rest = jnp.where(mask, jax.nn.log_sigmoid(-z), 0.0)
        after = lax.cumsum(log_rest, axis=3, reverse=True) - log_rest
        w = jnp.where(mask, jnp.exp(jax.nn.log_sigmoid(z) + after), 0.0)
        return jnp.einsum('bhts,bhsd->bhtd', w, vh)

    o = lax.map(block, (q_blocks, jnp.arange(n_blk) * Q_BLOCK))
    return o.transpose(1, 0, 3, 2, 4).reshape(B, S, D_GROUP).astype(q.dtype)


def _rwkv7(r, k, v, xw, xa, xg, w0, w2, a0, a2, g2, k_k, k_a, r_k, gn_g, gn_b):
    B, S, _ = r.shape
    f32 = jnp.float32
    w_log = -jax.nn.softplus(-(w0 + jnp.tanh(xw) @ w2).astype(f32)) - 0.5
    decay = jnp.exp(-jnp.exp(w_log))
    a = jax.nn.sigmoid((a0 + xa @ a2).astype(f32))
    g = (jax.nn.sigmoid(xg) @ g2).astype(f32)

    def heads(t):
        return t.astype(f32).reshape(B, S, RW_HEADS, RW_DIM)

    rh, kh, vh, dh, ah = heads(r), heads(k), heads(v), heads(decay), heads(a)
    kk = kh * k_k.astype(f32).reshape(RW_HEADS, RW_DIM)
    kk = kk * lax.rsqrt(jnp.maximum(jnp.sum(kk * kk, -1, keepdims=True), 1e-24))
    kh = kh * (1.0 + (ah - 1.0) * k_a.astype(f32).reshape(RW_HEADS, RW_DIM))

    def step(state, inp):
        rt, wt, kt, vt, kkt, at = inp
        removed = jnp.einsum('bhvk,bhk->bhv', state, kkt)
        state = (state * wt[:, :, None, :]
                 - removed[..., None] * (kkt * at)[:, :, None, :]
                 + vt[..., None] * kt[:, :, None, :])
        return state, jnp.einsum('bhvk,bhk->bhv', state, rt)

    s0 = jnp.zeros((B, RW_HEADS, RW_DIM, RW_DIM), f32)
    xs = tuple(jnp.moveaxis(t, 1, 0) for t in (rh, dh, kh, vh, kk, ah))
    _, out = lax.scan(step, s0, xs)
    out = jnp.moveaxis(out, 0, 1)
    mu = jnp.mean(out, -1, keepdims=True)
    var = jnp.mean(jnp.square(out - mu), -1, keepdims=True)
    out = ((out - mu) * lax.rsqrt(var + RW_GN_EPS)).reshape(B, S, D_GROUP)
    out = out * gn_g.astype(f32) + gn_b.astype(f32)
    bonus = jnp.sum(rh * kh * r_k.astype(f32), -1, keepdims=True) * vh
    out = (out + bonus.reshape(B, S, D_GROUP)) * g
    return out.astype(r.dtype)


def setup_inputs(seed: int = 0) -> dict:
    key = jax.random.key(seed)
    ks = iter(jax.random.split(key, 48))
    f32 = jnp.float32
    L = DEPTH

    def nrm(shape, scale):
        return scale * jax.random.normal(next(ks), shape, f32)

    n_idx = jnp.arange(S5_STATE, dtype=f32)
    return {
        'x': nrm((BATCH, SEQ, D_MODEL), 1.0),
        'c': nrm((BATCH, D_MODEL), 1.0),
        'ada_w': nrm((L, D_MODEL, 6 * D_MODEL), 0.5 * D_MODEL ** -0.5),
        'ada_b': nrm((L, 6 * D_MODEL), 0.01),
        'w_in': nrm((L, D_MODEL, N_IN), D_MODEL ** -0.5),
        'w_out': nrm((L, D_MIX, D_MODEL), DEEPNORM_BETA * D_MIX ** -0.5),
        'hg_lb_logits': nrm((L, D_GROUP), 0.5),
        'hg_norm_g': 1.0 + nrm((L, D_GROUP), 0.02),
        's5_a_re': -0.5 + nrm((L, S5_GROUPS, S5_STATE), 0.01),
        's5_a_im': math.pi * n_idx + nrm((L, S5_GROUPS, S5_STATE), 0.01),
        's5_log_dt': jax.random.uniform(next(ks), (L, S5_GROUPS), f32,
                                        math.log(S5_DT_MIN), math.log(S5_DT_MAX)),
        's5_b_re': nrm((L, S5_GROUPS, S5_STATE, S5_CH), (2 * S5_CH) ** -0.5),
        's5_b_im': nrm((L, S5_GROUPS, S5_STATE, S5_CH), (2 * S5_CH) ** -0.5),
        's5_c_re': nrm((L, S5_GROUPS, S5_CH, S5_STATE), S5_STATE ** -0.5),
        's5_c_im': nrm((L, S5_GROUPS, S5_CH, S5_STATE), S5_STATE ** -0.5),
        's5_d': nrm((L, D_GROUP), 1.0),
        's5_glu_w': nrm((L, D_GROUP, D_GROUP), D_GROUP ** -0.5),
        's5_glu_b': nrm((L, D_GROUP), 0.01),
        'rw_mu': jax.random.uniform(next(ks), (L, RW_WIDTH), f32),
        'rw_w0': nrm((L, D_GROUP), 1.0),
        'rw_w2': nrm((L, RW_W_RANK, D_GROUP), 0.1 * RW_W_RANK ** -0.5),
        'rw_a0': nrm((L, D_GROUP), 0.5),
        'rw_a2': nrm((L, RW_A_RANK, D_GROUP), 0.1 * RW_A_RANK ** -0.5),
        'rw_g2': nrm((L, RW_G_RANK, D_GROUP), RW_G_RANK ** -0.5),
        'rw_k_k': 0.85 + nrm((L, D_GROUP), 0.02),
        'rw_k_a': 1.0 + nrm((L, D_GROUP), 0.02),
        'rw_r_k': nrm((L, RW_HEADS, RW_DIM), 0.1),
        'rw_gn_g': 1.0 + nrm((L, D_GROUP), 0.02),
        'rw_gn_b': nrm((L, D_GROUP), 0.01),
        'ln1_g': 1.0 + nrm((L, D_MODEL), 0.02),
        'ln1_b': nrm((L, D_MODEL), 0.01),
        'ffn_w1': nrm((L, D_MODEL, D_FF), D_MODEL ** -0.5),
        'ffn_w3': nrm((L, D_MODEL, D_FF), D_MODEL ** -0.5),
        'ffn_w2': nrm((L, D_FF, D_MODEL), DEEPNORM_BETA * D_FF ** -0.5),
        'ln2_g': 1.0 + nrm((L, D_MODEL), 0.02),
        'ln2_b': nrm((L, D_MODEL), 0.01),
    }


def reference(x, c, ada_w, ada_b, w_in, w_out, hg_lb_logits, hg_norm_g,
              s5_a_re, s5_a_im, s5_log_dt, s5_b_re, s5_b_im, s5_c_re, s5_c_im,
              s5_d, s5_glu_w, s5_glu_b,
              rw_mu, rw_w0, rw_w2, rw_a0, rw_a2, rw_g2, rw_k_k, rw_k_a, rw_r_k,
              rw_gn_g, rw_gn_b,
              ln1_g, ln1_b, ffn_w1, ffn_w3, ffn_w2, ln2_g, ln2_b):
    lb_all = jnp.cumsum(jax.nn.softmax(hg_lb_logits.astype(jnp.float32), axis=0), axis=0)
    lb_all = lb_all - lb_all[:1]
    c_act = jax.nn.silu(c)
    for l in range(DEPTH):
        mod = c_act @ ada_w[l] + ada_b[l]
        shift1, scale1, gate1, shift2, scale2, gate2 = [m[:, None, :] for m in jnp.split(mod, 6, axis=-1)]

        h = x * (1.0 + scale1) + shift1
        proj = h @ w_in[l]
        hg_q, hg_f, hg_i, hg_g, s5_u, sb_q, sb_k, sb_v = jnp.split(proj[..., :RW_OFFSET], 8, axis=-1)
        rw = _token_shift(proj[..., RW_OFFSET:], rw_mu[l])
        rw_r, rw_k, rw_v, rw_xw, rw_xa, rw_xg = jnp.split(rw, RW_SPLITS, axis=-1)
        o_a = _hgrn2(hg_q, hg_f, hg_i, hg_g, lb_all[l], hg_norm_g[l])
        o_b = _s5(s5_u, s5_a_re[l], s5_a_im[l], s5_log_dt[l], s5_b_re[l], s5_b_im[l],
                  s5_c_re[l], s5_c_im[l], s5_d[l], s5_glu_w[l], s5_glu_b[l])
        o_c = _stick_breaking(sb_q, sb_k, sb_v)
        o_d = _rwkv7(rw_r, rw_k, rw_v, rw_xw, rw_xa, rw_xg, rw_w0[l], rw_w2[l], rw_a0[l],
                     rw_a2[l], rw_g2[l], rw_k_k[l], rw_k_a[l], rw_r_k[l], rw_gn_g[l], rw_gn_b[l])
        mix = jnp.concatenate([o_a, o_b, o_c, o_d], axis=-1) @ w_out[l]
        x = _layer_norm(DEEPNORM_ALPHA * x + gate1 * mix, ln1_g[l], ln1_b[l])

        h = x * (1.0 + scale2) + shift2
        ffn = (jax.nn.silu(h @ ffn_w1[l]) * (h @ ffn_w3[l])) @ ffn_w2[l]
        x = _layer_norm(DEEPNORM_ALPHA * x + gate2 * ffn, ln2_g[l], ln2_b[l])
    return x
```

```python
import functools
import math

import jax
import jax.numpy as jnp
from jax import lax
from jax.experimental import pallas as pl
from jax.experimental.pallas import tpu as pltpu

F32 = jnp.float32
BF16 = jnp.bfloat16

N_MIXERS = 4
CHUNK = 64
SUB = 16
HG_DIM = 128
S5_CH = 16
S5_STATE = 64
SB_DIM = 128
RW_DIM = 64
RW_W_RANK = 64
RW_A_RANK = 64
RW_G_RANK = 128
RW_GN_EPS = 64e-5
LN_EPS = 1e-5
RMS_EPS = 1e-6
V7X_VMEM_LIMIT = 56 * 1024 * 1024


def _params(sem, vmem=V7X_VMEM_LIMIT):
    return pltpu.CompilerParams(dimension_semantics=sem, vmem_limit_bytes=vmem)


def _dot(a, b):
    return jnp.dot(a, b, preferred_element_type=F32)


def _dot_nt(a, b):
    return lax.dot_general(a, b, (((1,), (1,)), ((), ())), preferred_element_type=F32)


def _dot_tn(a, b):
    return lax.dot_general(a, b, (((0,), (0,)), ((), ())), preferred_element_type=F32)


def _split3(x):
    h = x.astype(BF16)
    r = x - h.astype(F32)
    m = r.astype(BF16)
    lo = (r - m.astype(F32)).astype(BF16)
    return h, m, lo


def _dot_exact_lhs(a_bf16, x):
    h, m, lo = _split3(x)
    return _dot(a_bf16, h) + _dot(a_bf16, m) + _dot(a_bf16, lo)


def _dot_exact_rhs(x, b_bf16):
    h, m, lo = _split3(x)
    return _dot(h, b_bf16) + _dot(m, b_bf16) + _dot(lo, b_bf16)


def _softplus(x):
    return jnp.maximum(x, 0.0) + jnp.log1p(jnp.exp(-jnp.abs(x)))


def _sigmoid(x):
    return jax.nn.sigmoid(x)


def _layer_norm_rows(y, g, b):
    mu = jnp.mean(y, axis=-1, keepdims=True)
    d = y - mu
    var = jnp.mean(d * d, axis=-1, keepdims=True)
    return d * lax.rsqrt(var + LN_EPS) * g + b


def _ada_kernel(c_ref, w_ref, b_ref, o_ref):
    c = c_ref[...]
    ca = (c * _sigmoid(c)).astype(BF16)
    o_ref[0] = _dot(ca, w_ref[0].astype(BF16)) + b_ref[0]


def _ada_mod(c, ada_w, ada_b):
    depth, d, n = ada_w.shape
    bsz = c.shape[0]
    tn = 1024
    return pl.pallas_call(
        _ada_kernel,
        grid=(depth, n // tn),
        in_specs=[pl.BlockSpec((bsz, d), lambda l, j: (0, 0)),
                  pl.BlockSpec((1, d, tn), lambda l, j: (l, 0, j)),
                  pl.BlockSpec((1, 1, tn), lambda l, j: (l, 0, j))],
        out_specs=pl.BlockSpec((1, bsz, tn), lambda l, j: (l, 0, j)),
        out_shape=jax.ShapeDtypeStruct((depth, bsz, n), F32),
        compiler_params=_params(("parallel", "parallel")),
        name="ada_mod",
    )(c, ada_w, ada_b.reshape(depth, 1, n))


def _modulate_kernel(x_ref, sc_ref, sh_ref, o_ref):
    o_ref[...] = (x_ref[...] * (1.0 + sc_ref[0]) + sh_ref[0]).astype(o_ref.dtype)


def _modulate(x2, scale, shift, seq):
    t, d = x2.shape
    tm = min(1024, seq)
    per = seq // tm
    vec = pl.BlockSpec((1, 1, d), lambda i: (i // per, 0, 0))
    return pl.pallas_call(
        _modulate_kernel,
        grid=(t // tm,),
        in_specs=[pl.BlockSpec((tm, d), lambda i: (i, 0)), vec, vec],
        out_specs=pl.BlockSpec((tm, d), lambda i: (i, 0)),
        out_shape=jax.ShapeDtypeStruct((t, d), BF16),
        compiler_params=_params(("parallel",)),
        name="modulate",
    )(x2, scale, shift)


def _mm_kernel(a_ref, w_ref, o_ref):
    o_ref[...] = _dot(a_ref[...], w_ref[...]).astype(o_ref.dtype)


def _matmul(a, w, out_dtype, tm, tn):
    m, k = a.shape
    n = w.shape[1]
    tm = min(tm, m)
    return pl.pallas_call(
        _mm_kernel,
        grid=(m // tm, n // tn),
        in_specs=[pl.BlockSpec((tm, k), lambda i, j: (i, 0)),
                  pl.BlockSpec((k, tn), lambda i, j: (0, j))],
        out_specs=pl.BlockSpec((tm, tn), lambda i, j: (i, j)),
        out_shape=jax.ShapeDtypeStruct((m, n), out_dtype),
        compiler_params=_params(("parallel", "parallel")),
        name="in_proj",
    )(a, w)


def _ffn_up_kernel(a_ref, w1_ref, w3_ref, o_ref):
    a = a_ref[...]
    p1 = _dot(a, w1_ref[...])
    p3 = _dot(a, w3_ref[...])
    o_ref[...] = (p1 * _sigmoid(p1) * p3).astype(o_ref.dtype)


def _ffn_up(a, w1, w3, tm, tn):
    m, k = a.shape
    n = w1.shape[1]
    tm = min(tm, m)
    wspec = pl.BlockSpec((k, tn), lambda i, j: (0, j))
    return pl.pallas_call(
        _ffn_up_kernel,
        grid=(m // tm, n // tn),
        in_specs=[pl.BlockSpec((tm, k), lambda i, j: (i, 0)), wspec, wspec],
        out_specs=pl.BlockSpec((tm, tn), lambda i, j: (i, j)),
        out_shape=jax.ShapeDtypeStruct((m, n), BF16),
        compiler_params=_params(("parallel", "parallel")),
        name="ffn_up",
    )(a, w1, w3)


def _residual_norm(acc, x_ref, gate_ref, lng_ref, lnb_ref, sc_ref, sh_ref, xo_ref, ho_ref, alpha):
    y = alpha * x_ref[...] + gate_ref[0] * acc
    xn = _layer_norm_rows(y, lng_ref[...], lnb_ref[...])
    xo_ref[...] = xn
    ho_ref[...] = (xn * (1.0 + sc_ref[0]) + sh_ref[0]).astype(ho_ref.dtype)


def _out_proj_kernel(oa_ref, ob_ref, oc_ref, od_ref, w_ref, x_ref, gate_ref, lng_ref, lnb_ref, sc_ref, sh_ref,
                     xo_ref, ho_ref, *, alpha, dg):
    acc = _dot(oa_ref[...], w_ref[0 * dg:1 * dg, :])
    acc += _dot(ob_ref[...], w_ref[1 * dg:2 * dg, :])
    acc += _dot(oc_ref[...], w_ref[2 * dg:3 * dg, :])
    acc += _dot(od_ref[...], w_ref[3 * dg:4 * dg, :])
    _residual_norm(acc, x_ref, gate_ref, lng_ref, lnb_ref, sc_ref, sh_ref, xo_ref, ho_ref, alpha)


def _out_proj_norm(outs, w, x2, gate, lng, lnb, scale, shift, seq, alpha, tm):
    t, d = x2.shape
    dg = outs[0].shape[1]
    tm = min(tm, seq)
    per = seq // tm
    row = lambda i: (i, 0)
    vec = pl.BlockSpec((1, 1, d), lambda i: (i // per, 0, 0))
    par = pl.BlockSpec((1, d), lambda i: (0, 0))
    return pl.pallas_call(
        functools.partial(_out_proj_kernel, alpha=alpha, dg=dg),
        grid=(t // tm,),
        in_specs=[pl.BlockSpec((tm, dg), row)] * 4 + [pl.BlockSpec(w.shape, lambda i: (0, 0)),
                                                     pl.BlockSpec((tm, d), row), vec, par, par, vec, vec],
        out_specs=[pl.BlockSpec((tm, d), row), pl.BlockSpec((tm, d), row)],
        out_shape=[jax.ShapeDtypeStruct((t, d), F32), jax.ShapeDtypeStruct((t, d), BF16)],
        compiler_params=_params(("parallel",)),
        name="out_proj_norm",
    )(*outs, w, x2, gate, lng, lnb, scale, shift)


def _ffn_down_kernel(u_ref, w_ref, x_ref, gate_ref, lng_ref, lnb_ref, sc_ref, sh_ref, xo_ref, ho_ref, acc_ref,
                     *, alpha, nk):
    k = pl.program_id(1)

    @pl.when(k == 0)
    def _():
        acc_ref[...] = jnp.zeros_like(acc_ref)

    acc_ref[...] += _dot(u_ref[...], w_ref[...])

    @pl.when(k == nk - 1)
    def _():
        _residual_norm(acc_ref[...], x_ref, gate_ref, lng_ref, lnb_ref, sc_ref, sh_ref, xo_ref, ho_ref, alpha)


def _ffn_down_norm(u, w, x2, gate, lng, lnb, scale, shift, seq, alpha, tm, tk):
    t, d = x2.shape
    kk = u.shape[1]
    tm = min(tm, seq)
    per = seq // tm
    nk = kk // tk
    row = lambda i, k: (i, 0)
    vec = pl.BlockSpec((1, 1, d), lambda i, k: (i // per, 0, 0))
    par = pl.BlockSpec((1, d), lambda i, k: (0, 0))
    return pl.pallas_call(
        functools.partial(_ffn_down_kernel, alpha=alpha, nk=nk),
        grid=(t // tm, nk),
        in_specs=[pl.BlockSpec((tm, tk), lambda i, k: (i, k)), pl.BlockSpec((tk, d), lambda i, k: (k, 0)),
                  pl.BlockSpec((tm, d), row), vec, par, par, vec, vec],
        out_specs=[pl.BlockSpec((tm, d), row), pl.BlockSpec((tm, d), row)],
        out_shape=[jax.ShapeDtypeStruct((t, d), F32), jax.ShapeDtypeStruct((t, d), BF16)],
        scratch_shapes=[pltpu.VMEM((tm, d), F32)],
        compiler_params=_params(("parallel", "arbitrary")),
        name="ffn_down_norm",
    )(u, w, x2, gate, lng, lnb, scale, shift)


def _hgrn2_kernel(q_ref, f_ref, i_ref, g_ref, la_ref, lc_ref, oml_ref, ng_ref, tri_ref, o_ref, st_ref, *, n_heads, n_chunks):
    @pl.when(pl.program_id(1) == 0)
    def _():
        st_ref[...] = jnp.zeros_like(st_ref)

    la = la_ref[...]
    lc = lc_ref[...]
    oml = oml_ref[...]
    ng = ng_ref[...]
    tri = tri_ref[...]
    n_sub = CHUNK // SUB
    tio = lax.broadcasted_iota(jnp.int32, (n_sub, SUB, HG_DIM), 1)
    rio = lax.broadcasted_iota(jnp.int32, (CHUNK, HG_DIM), 0)

    def chunk_body(c, carry):
        rows = pl.ds(pl.multiple_of(c * CHUNK, CHUNK), CHUNK)
        z = f_ref[rows, :]
        log_sig = jnp.minimum(z, 0.0) - jnp.log1p(jnp.exp(-jnp.abs(z)))
        y = lc + log_sig
        mx = jnp.maximum(la, y)
        log_f = mx + jnp.log1p(jnp.exp(-jnp.abs(la - y)))
        kg = oml * _sigmoid(-z)
        q = q_ref[rows, :]
        qf = q * _sigmoid(q)
        v = i_ref[rows, :]
        g = g_ref[rows, :]
        b = _dot_exact_lhs(tri, log_f)
        eb = jnp.exp(b)
        b_last = b[CHUNK - 1:CHUNK, :]
        e_last = jnp.exp(b_last)
        k_end = kg * jnp.exp(b_last - b)
        q_in = qf * eb
        for h in range(n_heads):
            hs = slice(h * HG_DIM, (h + 1) * HG_DIM)
            bh, qh, kh, vh = b[:, hs], qf[:, hs], kg[:, hs], v[:, hs]
            vb = vh.astype(BF16)
            st = st_ref[h]
            o = _dot_nt(q_in[:, hs].astype(BF16), st.astype(BF16))
            b3 = bh.reshape(n_sub, SUB, HG_DIM)
            q3 = qh.reshape(n_sub, SUB, HG_DIM)
            k3 = kh.reshape(n_sub, SUB, HG_DIM)
            v3 = vh.reshape(n_sub, SUB, HG_DIM)
            acc = jnp.zeros((n_sub, SUB, HG_DIM), F32)
            for s in range(SUB):
                d = b3 - b3[:, s:s + 1, :]
                e = jnp.exp(jnp.where(tio >= s, d, -jnp.inf))
                w = jnp.sum(q3 * e * k3[:, s:s + 1, :], axis=-1, keepdims=True)
                acc = acc + w * v3[:, s:s + 1, :]
            o = o + acc.reshape(CHUNK, HG_DIM)
            parts = [jnp.zeros((SUB, HG_DIM), F32)]
            for i in range(1, n_sub):
                rho = bh[i * SUB - 1:i * SUB, :]
                qi = qh[i * SUB:(i + 1) * SUB, :] * jnp.exp(bh[i * SUB:(i + 1) * SUB, :] - rho)
                ki = kh * jnp.exp(jnp.where(rio < i * SUB, rho - bh, -jnp.inf))
                a = _dot_nt(qi.astype(BF16), ki.astype(BF16))
                parts.append(_dot(a.astype(BF16), vb))
            o = o + jnp.concatenate(parts, axis=0)
            st_ref[h] = st * e_last[:, hs] + _dot_tn(vb, k_end[:, hs].astype(BF16))
            o = o * lax.rsqrt(jnp.mean(o * o, axis=-1, keepdims=True) + RMS_EPS) * ng[:, hs]
            gh = g[:, hs]
            o_ref[rows, hs] = (o * (gh * _sigmoid(gh))).astype(o_ref.dtype)
        return carry

    lax.fori_loop(0, n_chunks, chunk_body, 0)


def _hgrn2(proj, lb, norm_g, bsz, seq, dg, tb):
    n_heads = dg // HG_DIM
    tb = min(tb, seq)
    nt = seq // tb
    lbf = lb.astype(F32).reshape(1, dg)
    la = jnp.log(jnp.maximum(lbf, 0.0))
    lc = jnp.log1p(-lbf)
    oml = 1.0 - lbf
    r = jnp.arange(CHUNK)
    tri = (r[:, None] >= r[None, :]).astype(BF16)
    col = lambda j: pl.BlockSpec((tb, dg), lambda b, t, j=j: (b * nt + t, j))
    par = pl.BlockSpec((1, dg), lambda b, t: (0, 0))
    return pl.pallas_call(
        functools.partial(_hgrn2_kernel, n_heads=n_heads, n_chunks=tb // CHUNK),
        grid=(bsz, nt),
        in_specs=[col(0), col(1), col(2), col(3), par, par, par, par,
                  pl.BlockSpec((CHUNK, CHUNK), lambda b, t: (0, 0))],
        out_specs=pl.BlockSpec((tb, dg), lambda b, t: (b * nt + t, 0)),
        out_shape=jax.ShapeDtypeStruct((bsz * seq, dg), BF16),
        scratch_shapes=[pltpu.VMEM((n_heads, HG_DIM, HG_DIM), F32)],
        compiler_params=_params(("parallel", "arbitrary")),
        name="hgrn2",
    )(proj, proj, proj, proj, la, lc, oml, norm_g.astype(F32).reshape(1, dg), tri)


def _s5_kernel(u_ref, bbar_ref, lam_ref, cmat_ref, dskip_ref, gw_ref, gb_ref, o_ref, xs_ref, h_ref, *, tb, bsz, ns, lane_chunk):
    @pl.when(pl.program_id(0) == 0)
    def _():
        h_ref[...] = jnp.zeros_like(h_ref)

    u = u_ref[...]
    xs_ref[...] = _dot(u.astype(BF16), bbar_ref[...])
    for c in range(ns // lane_chunk):
        re = slice(c * lane_chunk, (c + 1) * lane_chunk)
        im = slice(ns + c * lane_chunk, ns + (c + 1) * lane_chunk)
        lr = jnp.broadcast_to(lam_ref[0:1, re], (bsz, lane_chunk))
        li = jnp.broadcast_to(lam_ref[1:2, re], (bsz, lane_chunk))

        def step(t, carry):
            hr, hi = carry
            rows = pl.ds(pl.multiple_of(t * bsz, bsz), bsz)
            nr = lr * hr - li * hi + xs_ref[rows, re]
            ni = lr * hi + li * hr + xs_ref[rows, im]
            xs_ref[rows, re] = nr
            xs_ref[rows, im] = ni
            return nr, ni

        hr, hi = lax.fori_loop(0, tb, step, (h_ref[:, re], h_ref[:, im]), unroll=4)
        h_ref[:, re] = hr
        h_ref[:, im] = hi
    y = _dot(xs_ref[...].astype(BF16), cmat_ref[...]) + dskip_ref[...] * u
    y = 0.5 * y * (1.0 + jnp.tanh(math.sqrt(2.0 / math.pi) * (y + 0.044715 * (y * y * y))))
    gate = _sigmoid(_dot(y.astype(BF16), gw_ref[...]) + gb_ref[...])
    o_ref[...] = (y * gate).astype(o_ref.dtype)


def _s5(u_tm, a_re, a_im, log_dt, b_re, b_im, c_re, c_im, d_skip, glu_w, glu_b, bsz, tb):
    dg = u_tm.shape[1]
    seq = u_tm.shape[0] // bsz
    groups, p = a_re.shape
    ch = dg // groups
    ns = groups * p
    dt = jnp.exp(log_dt.astype(F32))[:, None]
    ar, ai = a_re.astype(F32), a_im.astype(F32)
    mag = jnp.exp(ar * dt)
    lr, li = mag * jnp.cos(ai * dt), mag * jnp.sin(ai * dt)
    den = ar * ar + ai * ai
    zr = ((lr - 1.0) * ar + li * ai) / den
    zi = (li * ar - (lr - 1.0) * ai) / den
    br, bi = b_re.astype(F32), b_im.astype(F32)
    bbar_r = zr[..., None] * br - zi[..., None] * bi
    bbar_i = zr[..., None] * bi + zi[..., None] * br
    eye = jnp.eye(groups, dtype=F32)

    def blockdiag_in(w):
        return jnp.einsum('gpc,gh->gchp', w, eye).reshape(groups * ch, groups * p)

    def blockdiag_out(w):
        return jnp.einsum('gcp,gh->gphc', w, eye).reshape(groups * p, groups * ch)

    bbar = jnp.concatenate([blockdiag_in(bbar_r), blockdiag_in(bbar_i)], axis=1).astype(BF16)
    cmat = jnp.concatenate([blockdiag_out(c_re.astype(F32)), -blockdiag_out(c_im.astype(F32))], axis=0).astype(BF16)
    lam = jnp.stack([lr.reshape(ns), li.reshape(ns)], axis=0)
    tb = min(tb, seq)
    full = lambda a: pl.BlockSpec(a.shape, lambda t: (0,) * a.ndim)
    dsk = d_skip.astype(F32).reshape(1, dg)
    gw = glu_w.astype(BF16)
    gb = glu_b.astype(F32).reshape(1, dg)
    return pl.pallas_call(
        functools.partial(_s5_kernel, tb=tb, bsz=bsz, ns=ns, lane_chunk=512),
        grid=(seq // tb,),
        in_specs=[pl.BlockSpec((tb * bsz, dg), lambda t: (t, 0)), full(bbar), full(lam), full(cmat), full(dsk),
                  full(gw), full(gb)],
        out_specs=pl.BlockSpec((tb * bsz, dg), lambda t: (t, 0)),
        out_shape=jax.ShapeDtypeStruct((seq * bsz, dg), BF16),
        scratch_shapes=[pltpu.VMEM((tb * bsz, 2 * ns), F32), pltpu.VMEM((bsz, 2 * ns), F32)],
        compiler_params=_params(("arbitrary",)),
        name="s5",
    )(u_tm, bbar, lam, cmat, dsk, gw, gb)


def _sb_kernel(q_ref, k_ref, v_ref, tri_ref, o_ref, kb_ref, vb_ref, *, tq, scale):
    qi = pl.program_id(2)

    @pl.when(qi == 0)
    def _():
        kb_ref[...] = k_ref[...].astype(BF16)
        vb_ref[...] = v_ref[...].astype(BF16)

    q = (q_ref[...] * scale).astype(BF16)
    tri = tri_ref[...]
    row = lax.broadcasted_iota(jnp.int32, (tq, tq), 0)
    col = lax.broadcasted_iota(jnp.int32, (tq, tq), 1)
    below = col < row

    def tile(kb, carry, acc, diag):
        rows = pl.ds(pl.multiple_of(kb * tq, tq), tq)
        z = _dot_nt(q, kb_ref[rows, :])
        sp = _softplus(z)
        log_rest = -sp
        if diag:
            log_rest = jnp.where(below, log_rest, 0.0)
        after = _dot_exact_rhs(log_rest, tri)
        lw = (z - sp) + after + carry
        w = jnp.exp(lw)
        if diag:
            w = jnp.where(below, w, 0.0)
        acc = acc + _dot(w.astype(BF16), vb_ref[rows, :])
        carry = carry + jnp.sum(log_rest, axis=-1, keepdims=True)
        return carry, acc

    carry, acc = tile(qi, jnp.zeros((tq, 1), F32), jnp.zeros((tq, SB_DIM), F32), True)

    def body(j, c):
        return tile(qi - 1 - j, c[0], c[1], False)

    carry, acc = lax.fori_loop(0, qi, body, (carry, acc))
    o_ref[...] = acc.astype(o_ref.dtype)


def _stick_breaking(proj, bsz, seq, dg, col0, tq):
    n_heads = dg // SB_DIM
    tq = min(tq, seq)
    nq = seq // tq
    r = jnp.arange(tq)
    tri = (r[:, None] > r[None, :]).astype(BF16)
    cb = col0 // SB_DIM
    return pl.pallas_call(
        functools.partial(_sb_kernel, tq=tq, scale=SB_DIM ** -0.5),
        grid=(bsz, n_heads, nq),
        in_specs=[pl.BlockSpec((tq, SB_DIM), lambda b, h, i: (b * nq + i, cb + h)),
                  pl.BlockSpec((seq, SB_DIM), lambda b, h, i: (b, cb + n_heads + h)),
                  pl.BlockSpec((seq, SB_DIM), lambda b, h, i: (b, cb + 2 * n_heads + h)),
                  pl.BlockSpec((tq, tq), lambda b, h, i: (0, 0))],
        out_specs=pl.BlockSpec((tq, SB_DIM), lambda b, h, i: (b * nq + i, h)),
        out_shape=jax.ShapeDtypeStruct((bsz * seq, dg), BF16),
        scratch_shapes=[pltpu.VMEM((seq, SB_DIM), BF16), pltpu.VMEM((seq, SB_DIM), BF16)],
        compiler_params=_params(("parallel", "parallel", "arbitrary")),
        name="stick_breaking",
    )(proj, proj, proj, tri)


def _rwkv7_kernel(r_ref, k_ref, v_ref, tail_ref, mu_r_ref, mu_k_ref, mu_v_ref, mu_t_ref, w0_ref, w2_ref, a0_ref,
                  a2_ref, g2_ref, kk_ref, ka_ref, rk_ref, gng_ref, gnb_ref, tri_ref, seg_ref, o_ref,
                  mt_ref, prev_ref, prev_t_ref, *, n_pairs):
    dg = r_ref.shape[-1]

    @pl.when(pl.program_id(1) == 0)
    def _():
        mt_ref[...] = jnp.zeros_like(mt_ref)
        prev_ref[...] = jnp.zeros_like(prev_ref)
        prev_t_ref[...] = jnp.zeros_like(prev_t_ref)

    def shifted(p, prev_row, mu):
        rolled = pltpu.roll(p, 1, axis=0)
        first = lax.broadcasted_iota(jnp.int32, p.shape, 0) == 0
        prev = jnp.where(first, prev_row, rolled)
        return p + mu * (prev - p)

    pr, pk, pv, pt = r_ref[...], k_ref[...], v_ref[...], tail_ref[...]
    r = shifted(pr, prev_ref[0:1, :], mu_r_ref[...])
    k = shifted(pk, prev_ref[1:2, :], mu_k_ref[...])
    v = shifted(pv, prev_ref[2:3, :], mu_v_ref[...])
    tail = shifted(pt, prev_t_ref[...], mu_t_ref[...])
    prev_ref[0:1, :] = pr[CHUNK - 1:CHUNK, :]
    prev_ref[1:2, :] = pk[CHUNK - 1:CHUNK, :]
    prev_ref[2:3, :] = pv[CHUNK - 1:CHUNK, :]
    prev_t_ref[...] = pt[CHUNK - 1:CHUNK, :]

    xw = tail[:, 0:RW_W_RANK]
    xa = tail[:, RW_W_RANK:RW_W_RANK + RW_A_RANK]
    xg = tail[:, RW_W_RANK + RW_A_RANK:]
    w_log = -_softplus(-(w0_ref[...] + _dot(jnp.tanh(xw).astype(BF16), w2_ref[...]))) - 0.5
    logw = -jnp.exp(w_log)
    a = _sigmoid(a0_ref[...] + _dot(xa.astype(BF16), a2_ref[...]))
    g = _dot(_sigmoid(xg).astype(BF16), g2_ref[...])
    kk = k * kk_ref[...]
    seg = seg_ref[...]
    kk = kk * lax.rsqrt(jnp.maximum(_dot_exact_rhs(kk * kk, seg), 1e-24))
    k2 = k * (1.0 + (a - 1.0) * ka_ref[...])
    beta = kk * a
    lp = _dot_exact_lhs(tri_ref[...], logw)
    lp_end = lp[CHUNK - 1:CHUNK, :]
    e_pos = jnp.exp(lp)
    e_neg = jnp.exp(-lp)
    e_end = jnp.exp(lp_end - lp)
    r_b = r * e_pos
    a_b = kk * jnp.exp(lp - logw)
    k_b = k2 * e_neg
    b_b = beta * e_neg
    k_t = k2 * e_end
    b_t = beta * e_end
    bonus = _dot_exact_rhs(r * k2 * rk_ref[...], seg) * v

    lane = lax.broadcasted_iota(jnp.int32, (CHUNK, 2 * RW_DIM), 1)
    head0 = lane < RW_DIM
    ri = lax.broadcasted_iota(jnp.int32, (2 * CHUNK, 2 * CHUNK), 0)
    ci = lax.broadcasted_iota(jnp.int32, (2 * CHUNK, 2 * CHUNK), 1)
    same = (ri < CHUNK) == (ci < CHUNK)
    strict = jnp.logical_and(same, ci < ri)
    incl = jnp.logical_and(same, ci <= ri)
    eye = (ri == ci).astype(F32)

    def stack(x):
        return jnp.concatenate([jnp.where(head0, x, 0.0), jnp.where(head0, 0.0, x)], axis=0)

    def unstack(x):
        return x[:CHUNK, :] + x[CHUNK:, :]

    for p in range(n_pairs):
        ps = slice(p * 2 * RW_DIM, (p + 1) * 2 * RW_DIM)
        mt = mt_ref[p]
        mtb = mt.astype(BF16)
        a_s = stack(a_b[:, ps]).astype(BF16)
        r_s = stack(r_b[:, ps]).astype(BF16)
        kb_s = stack(k_b[:, ps]).astype(BF16)
        bb_s = stack(b_b[:, ps]).astype(BF16)
        v_s = stack(v[:, ps]).astype(BF16)
        l_k = jnp.where(strict, _dot_nt(a_s, kb_s), 0.0)
        l_b = jnp.where(strict, _dot_nt(a_s, bb_s), 0.0)
        g_k = jnp.where(incl, _dot_nt(r_s, kb_s), 0.0)
        g_b = jnp.where(incl, _dot_nt(r_s, bb_s), 0.0)
        rhs = _dot_nt(a_s, mtb) + _dot(l_k.astype(BF16), v_s)
        x = -l_b
        t_inv = eye + x
        for _ in range(int(math.log2(CHUNK)) - 1):
            x = _dot(x.astype(BF16), x.astype(BF16))
            t_inv = t_inv + _dot(t_inv.astype(BF16), x.astype(BF16))
        u_s = _dot(t_inv.astype(BF16), rhs.astype(BF16))
        u_sb = u_s.astype(BF16)
        o_s = _dot_nt(r_s, mtb) + _dot(g_k.astype(BF16), v_s) - _dot(g_b.astype(BF16), u_sb)
        kt_s = stack(k_t[:, ps]).astype(BF16)
        bt_s = stack(b_t[:, ps]).astype(BF16)
        mt_ref[p] = mt * e_pos[CHUNK - 1:CHUNK, ps] + _dot_tn(v_s, kt_s) - _dot_tn(u_sb, bt_s)
        out = unstack(o_s)
        segp = seg[ps, ps]
        mean = _dot_exact_rhs(out, segp) * (1.0 / RW_DIM)
        d = out - mean
        var = _dot_exact_rhs(d * d, segp) * (1.0 / RW_DIM)
        out = d * lax.rsqrt(var + RW_GN_EPS) * gng_ref[:, ps] + gnb_ref[:, ps]
        o_ref[:, ps] = ((out + bonus[:, ps]) * g[:, ps]).astype(o_ref.dtype)


def _rwkv7(proj, mu, w0, w2, a0, a2, g2, k_k, k_a, r_k, gn_g, gn_b, bsz, seq, dg, col0):
    n_heads = dg // RW_DIM
    n_pairs = n_heads // 2
    nt = seq // CHUNK
    tail_w = RW_W_RANK + RW_A_RANK + RW_G_RANK
    cb = col0 // dg
    tb_col = (col0 + 3 * dg) // tail_w
    muf = mu.astype(F32)
    vec = lambda a: a.astype(F32).reshape(1, -1)
    rr = jnp.arange(CHUNK)
    tri = (rr[:, None] >= rr[None, :]).astype(BF16)
    hh = jnp.arange(dg) // RW_DIM
    seg = (hh[:, None] == hh[None, :]).astype(BF16)
    rowblk = lambda j: pl.BlockSpec((CHUNK, dg), lambda b, t, j=j: (b * nt + t, cb + j))
    full = lambda a: pl.BlockSpec(a.shape, lambda b, t: (0,) * a.ndim)
    args = [vec(muf[0:dg]), vec(muf[dg:2 * dg]), vec(muf[2 * dg:3 * dg]), vec(muf[3 * dg:]),
            vec(w0), w2.astype(BF16), vec(a0), a2.astype(BF16), g2.astype(BF16), vec(k_k), vec(k_a), vec(r_k),
            vec(gn_g), vec(gn_b), tri, seg]
    return pl.pallas_call(
        functools.partial(_rwkv7_kernel, n_pairs=n_pairs),
        grid=(bsz, nt),
        in_specs=[rowblk(0), rowblk(1), rowblk(2),
                  pl.BlockSpec((CHUNK, tail_w), lambda b, t: (b * nt + t, tb_col))] + [full(a) for a in args],
        out_specs=pl.BlockSpec((CHUNK, dg), lambda b, t: (b * nt + t, 0)),
        out_shape=jax.ShapeDtypeStruct((bsz * seq, dg), BF16),
        scratch_shapes=[pltpu.VMEM((n_pairs, 2 * RW_DIM, 2 * RW_DIM), F32), pltpu.VMEM((8, dg), F32),
                        pltpu.VMEM((1, tail_w), F32)],
        compiler_params=_params(("parallel", "arbitrary")),
        name="rwkv7",
    )(proj, proj, proj, proj, *args)


def kernel(x, c, ada_w, ada_b, w_in, w_out, hg_lb_logits, hg_norm_g, s5_a_re, s5_a_im, s5_log_dt, s5_b_re, s5_b_im, s5_c_re, s5_c_im, s5_d, s5_glu_w, s5_glu_b, rw_mu, rw_w0, rw_w2, rw_a0, rw_a2, rw_g2, rw_k_k, rw_k_a, rw_r_k, rw_gn_g, rw_gn_b, ln1_g, ln1_b, ffn_w1, ffn_w3, ffn_w2, ln2_g, ln2_b):
    bsz, seq, d = x.shape
    depth = w_in.shape[0]
    n_in = w_in.shape[2]
    dg = d // N_MIXERS
    d_ff = ffn_w1.shape[2]
    alpha = (2 * depth) ** 0.25
    tokens = bsz * seq

    lb_all = jnp.cumsum(jax.nn.softmax(hg_lb_logits.astype(F32), axis=0), axis=0)
    lb_all = lb_all - lb_all[:1]

    mod = _ada_mod(c.astype(F32), ada_w, ada_b)
    mod = mod.reshape(depth, bsz, 6, 1, d)
    shift1, scale1, gate1, shift2, scale2, gate2 = [mod[:, :, i] for i in range(6)]
    zeros_vec = jnp.zeros((bsz, 1, d), F32)

    n_pad = -(-n_in // 1024) * 1024
    x2 = x.reshape(tokens, d)
    h = _modulate(x2, scale1[0], shift1[0], seq)
    for l in range(depth):
        w_in_l = jnp.pad(w_in[l].astype(BF16), ((0, 0), (0, n_pad - n_in)))
        proj = _matmul(h, w_in_l, F32, 1024, 1024)
        o_a = _hgrn2(proj, lb_all[l], hg_norm_g[l], bsz, seq, dg, 256)
        u_tm = proj.reshape(bsz, seq, n_pad)[:, :, 4 * dg:5 * dg].transpose(1, 0, 2).reshape(tokens, dg)
        o_b = _s5(u_tm, s5_a_re[l], s5_a_im[l], s5_log_dt[l], s5_b_re[l], s5_b_im[l], s5_c_re[l], s5_c_im[l],
                  s5_d[l], s5_glu_w[l], s5_glu_b[l], bsz, 64)
        o_b = o_b.reshape(seq, bsz, dg).transpose(1, 0, 2).reshape(tokens, dg)
        o_c = _stick_breaking(proj, bsz, seq, dg, 5 * dg, 256)
        o_d = _rwkv7(proj, rw_mu[l], rw_w0[l], rw_w2[l], rw_a0[l], rw_a2[l], rw_g2[l], rw_k_k[l], rw_k_a[l],
                     rw_r_k[l].reshape(-1), rw_gn_g[l], rw_gn_b[l], bsz, seq, dg, 8 * dg)
        x2, h = _out_proj_norm([o_a, o_b, o_c, o_d], w_out[l].astype(BF16), x2, gate1[l],
                               ln1_g[l].astype(F32).reshape(1, d), ln1_b[l].astype(F32).reshape(1, d),
                               scale2[l], shift2[l], seq, alpha, 256)
        u = _ffn_up(h, ffn_w1[l].astype(BF16), ffn_w3[l].astype(BF16), 1024, 512)
        last = l == depth - 1
        x2, h = _ffn_down_norm(u, ffn_w2[l].astype(BF16), x2, gate2[l],
                               ln2_g[l].astype(F32).reshape(1, d), ln2_b[l].astype(F32).reshape(1, d),
                               zeros_vec if last else scale1[l + 1], zeros_vec if last else shift1[l + 1],
                               seq, alpha, 512, 512)
    return x2.reshape(bsz, seq, d)
```

```python
import functools
import math

import jax
import jax.numpy as jnp
from jax import lax
from jax.experimental import pallas as pl
from jax.experimental.pallas import tpu as pltpu

F32 = jnp.float32
BF16 = jnp.bfloat16

N_MIXERS = 4
CHUNK = 64
SUB = 16
HG_DIM = 128
S5_CH = 16
S5_STATE = 64
SB_DIM = 128
RW_DIM = 64
RW_W_RANK = 64
RW_A_RANK = 64
RW_G_RANK = 128
RW_GN_EPS = 64e-5
LN_EPS = 1e-5
RMS_EPS = 1e-6
V7X_VMEM_LIMIT = 56 * 1024 * 1024


def _params(sem, vmem=V7X_VMEM_LIMIT):
    return pltpu.CompilerParams(dimension_semantics=sem, vmem_limit_bytes=vmem)


def _dot(a, b):
    return jnp.dot(a, b, preferred_element_type=F32)


def _dot_nt(a, b):
    return lax.dot_general(a, b, (((1,), (1,)), ((), ())), preferred_element_type=F32)


def _dot_tn(a, b):
    return lax.dot_general(a, b, (((0,), (0,)), ((), ())), preferred_element_type=F32)


def _split3(x):
    h = x.astype(BF16)
    r = x - h.astype(F32)
    m = r.astype(BF16)
    lo = (r - m.astype(F32)).astype(BF16)
    return h, m, lo


def _dot_exact_lhs(a_bf16, x):
    h, m, lo = _split3(x)
    return _dot(a_bf16, h) + _dot(a_bf16, m) + _dot(a_bf16, lo)


def _dot_exact_rhs(x, b_bf16):
    h, m, lo = _split3(x)
    return _dot(h, b_bf16) + _dot(m, b_bf16) + _dot(lo, b_bf16)


def _softplus(x):
    return jnp.maximum(x, 0.0) + jnp.log1p(jnp.exp(-jnp.abs(x)))


def _sigmoid(x):
    return jax.nn.sigmoid(x)


def _layer_norm_rows(y, g, b):
    mu = jnp.mean(y, axis=-1, keepdims=True)
    d = y - mu
    var = jnp.mean(d * d, axis=-1, keepdims=True)
    return d * lax.rsqrt(var + LN_EPS) * g + b


def _ada_kernel(c_ref, w_ref, b_ref, o_ref):
    c = c_ref[...]
    ca = (c * _sigmoid(c)).astype(BF16)
    o_ref[0] = _dot(ca, w_ref[0].astype(BF16)) + b_ref[0]


def _ada_mod(c, ada_w, ada_b):
    depth, d, n = ada_w.shape
    bsz = c.shape[0]
    tn = 1024
    return pl.pallas_call(
        _ada_kernel,
        grid=(depth, n // tn),
        in_specs=[pl.BlockSpec((bsz, d), lambda l, j: (0, 0)),
                  pl.BlockSpec((1, d, tn), lambda l, j: (l, 0, j)),
                  pl.BlockSpec((1, 1, tn), lambda l, j: (l, 0, j))],
        out_specs=pl.BlockSpec((1, bsz, tn), lambda l, j: (l, 0, j)),
        out_shape=jax.ShapeDtypeStruct((depth, bsz, n), F32),
        compiler_params=_params(("parallel", "parallel")),
        name="ada_mod",
    )(c, ada_w, ada_b.reshape(depth, 1, n))


def _modulate_kernel(x_ref, sc_ref, sh_ref, o_ref):
    o_ref[...] = (x_ref[...] * (1.0 + sc_ref[0]) + sh_ref[0]).astype(o_ref.dtype)


def _modulate(x2, scale, shift, seq):
    t, d = x2.shape
    tm = min(1024, seq)
    per = seq // tm
    vec = pl.BlockSpec((1, 1, d), lambda i: (i // per, 0, 0))
    return pl.pallas_call(
        _modulate_kernel,
        grid=(t // tm,),
        in_specs=[pl.BlockSpec((tm, d), lambda i: (i, 0)), vec, vec],
        out_specs=pl.BlockSpec((tm, d), lambda i: (i, 0)),
        out_shape=jax.ShapeDtypeStruct((t, d), BF16),
        compiler_params=_params(("parallel",)),
        name="modulate",
    )(x2, scale, shift)


def _mm_kernel(a_ref, w_ref, o_ref):
    o_ref[...] = _dot(a_ref[...], w_ref[...]).astype(o_ref.dtype)


def _matmul(a, w, out_dtype, tm, tn):
    m, k = a.shape
    n = w.shape[1]
    tm = min(tm, m)
    return pl.pallas_call(
        _mm_kernel,
        grid=(m // tm, n // tn),
        in_specs=[pl.BlockSpec((tm, k), lambda i, j: (i, 0)),
                  pl.BlockSpec((k, tn), lambda i, j: (0, j))],
        out_specs=pl.BlockSpec((tm, tn), lambda i, j: (i, j)),
        out_shape=jax.ShapeDtypeStruct((m, n), out_dtype),
        compiler_params=_params(("parallel", "parallel")),
        name="in_proj",
    )(a, w)


def _ffn_up_kernel(a_ref, w1_ref, w3_ref, o_ref):
    a = a_ref[...]
    p1 = _dot(a, w1_ref[...])
    p3 = _dot(a, w3_ref[...])
    o_ref[...] = (p1 * _sigmoid(p1) * p3).astype(o_ref.dtype)


def _ffn_up(a, w1, w3, tm, tn):
    m, k = a.shape
    n = w1.shape[1]
    tm = min(tm, m)
    wspec = pl.BlockSpec((k, tn), lambda i, j: (0, j))
    return pl.pallas_call(
        _ffn_up_kernel,
        grid=(m // tm, n // tn),
        in_specs=[pl.BlockSpec((tm, k), lambda i, j: (i, 0)), wspec, wspec],
        out_specs=pl.BlockSpec((tm, tn), lambda i, j: (i, j)),
        out_shape=jax.ShapeDtypeStruct((m, n), BF16),
        compiler_params=_params(("parallel", "parallel")),
        name="ffn_up",
    )(a, w1, w3)


def _residual_norm(acc, x_ref, gate_ref, lng_ref, lnb_ref, sc_ref, sh_ref, xo_ref, ho_ref, alpha):
    y = alpha * x_ref[...] + gate_ref[0] * acc
    xn = _layer_norm_rows(y, lng_ref[...], lnb_ref[...])
    xo_ref[...] = xn
    ho_ref[...] = (xn * (1.0 + sc_ref[0]) + sh_ref[0]).astype(ho_ref.dtype)


def _out_proj_kernel(oa_ref, ob_ref, oc_ref, od_ref, w_ref, x_ref, gate_ref, lng_ref, lnb_ref, sc_ref, sh_ref,
                     xo_ref, ho_ref, *, alpha, dg):
    acc = _dot(oa_ref[...], w_ref[0 * dg:1 * dg, :])
    acc += _dot(ob_ref[...], w_ref[1 * dg:2 * dg, :])
    acc += _dot(oc_ref[...], w_ref[2 * dg:3 * dg, :])
    acc += _dot(od_ref[...], w_ref[3 * dg:4 * dg, :])
    _residual_norm(acc, x_ref, gate_ref, lng_ref, lnb_ref, sc_ref, sh_ref, xo_ref, ho_ref, alpha)


def _out_proj_norm(outs, w, x2, gate, lng, lnb, scale, shift, seq, alpha, tm):
    t, d = x2.shape
    dg = outs[0].shape[1]
    tm = min(tm, seq)
    per = seq // tm
    row = lambda i: (i, 0)
    vec = pl.BlockSpec((1, 1, d), lambda i: (i // per, 0, 0))
    par = pl.BlockSpec((1, d), lambda i: (0, 0))
    return pl.pallas_call(
        functools.partial(_out_proj_kernel, alpha=alpha, dg=dg),
        grid=(t // tm,),
        in_specs=[pl.BlockSpec((tm, dg), row)] * 4 + [
            pl.BlockSpec(w.shape, lambda i: (0, 0), pipeline_mode=pl.Buffered(1)),
            pl.BlockSpec((tm, d), row), vec, par, par, vec, vec],
        out_specs=[pl.BlockSpec((tm, d), row), pl.BlockSpec((tm, d), row)],
        out_shape=[jax.ShapeDtypeStruct((t, d), F32), jax.ShapeDtypeStruct((t, d), BF16)],
        compiler_params=_params(("parallel",)),
        name="out_proj_norm",
    )(*outs, w, x2, gate, lng, lnb, scale, shift)


def _ffn_down_kernel(u_ref, w_ref, x_ref, gate_ref, lng_ref, lnb_ref, sc_ref, sh_ref, xo_ref, ho_ref, *, alpha):
    acc = _dot(u_ref[...], w_ref[...])
    _residual_norm(acc, x_ref, gate_ref, lng_ref, lnb_ref, sc_ref, sh_ref, xo_ref, ho_ref, alpha)


def _ffn_down_norm(u, w, x2, gate, lng, lnb, scale, shift, seq, alpha, tm):
    t, d = x2.shape
    kk = u.shape[1]
    tm = min(tm, seq)
    per = seq // tm
    row = lambda i: (i, 0)
    vec = pl.BlockSpec((1, 1, d), lambda i: (i // per, 0, 0))
    par = pl.BlockSpec((1, d), lambda i: (0, 0))
    return pl.pallas_call(
        functools.partial(_ffn_down_kernel, alpha=alpha),
        grid=(t // tm,),
        in_specs=[pl.BlockSpec((tm, kk), row),
                  pl.BlockSpec((kk, d), lambda i: (0, 0), pipeline_mode=pl.Buffered(1)),
                  pl.BlockSpec((tm, d), row), vec, par, par, vec, vec],
        out_specs=[pl.BlockSpec((tm, d), row), pl.BlockSpec((tm, d), row)],
        out_shape=[jax.ShapeDtypeStruct((t, d), F32), jax.ShapeDtypeStruct((t, d), BF16)],
        compiler_params=_params(("parallel",)),
        name="ffn_down_norm",
    )(u, w, x2, gate, lng, lnb, scale, shift)


def _hgrn2_kernel(q_ref, f_ref, i_ref, g_ref, la_ref, lc_ref, oml_ref, ng_ref, tri_ref, o_ref, st_ref, *, n_heads, n_chunks):
    @pl.when(pl.program_id(1) == 0)
    def _():
        st_ref[...] = jnp.zeros_like(st_ref)

    la = la_ref[...]
    lc = lc_ref[...]
    oml = oml_ref[...]
    ng = ng_ref[...]
    tri = tri_ref[...]
    n_sub = CHUNK // SUB
    tio = lax.broadcasted_iota(jnp.int32, (n_sub, SUB, HG_DIM), 1)
    rio = lax.broadcasted_iota(jnp.int32, (CHUNK, HG_DIM), 0)

    def chunk_body(c, carry):
        rows = pl.ds(pl.multiple_of(c * CHUNK, CHUNK), CHUNK)
        z = f_ref[rows, :]
        log_sig = jnp.minimum(z, 0.0) - jnp.log1p(jnp.exp(-jnp.abs(z)))
        y = lc + log_sig
        mx = jnp.maximum(la, y)
        log_f = mx + jnp.log1p(jnp.exp(-jnp.abs(la - y)))
        kg = oml * _sigmoid(-z)
        q = q_ref[rows, :]
        qf = q * _sigmoid(q)
        v = i_ref[rows, :]
        g = g_ref[rows, :]
        b = _dot_exact_lhs(tri, log_f)
        eb = jnp.exp(b)
        b_last = b[CHUNK - 1:CHUNK, :]
        e_last = jnp.exp(b_last)
        k_end = kg * jnp.exp(b_last - b)
        q_in = qf * eb
        for h in range(n_heads):
            hs = slice(h * HG_DIM, (h + 1) * HG_DIM)
            bh, qh, kh, vh = b[:, hs], qf[:, hs], kg[:, hs], v[:, hs]
            vb = vh.astype(BF16)
            st = st_ref[h]
            o = _dot_nt(q_in[:, hs].astype(BF16), st.astype(BF16))
            b3 = bh.reshape(n_sub, SUB, HG_DIM)
            q3 = qh.reshape(n_sub, SUB, HG_DIM)
            k3 = kh.reshape(n_sub, SUB, HG_DIM)
            v3 = vh.reshape(n_sub, SUB, HG_DIM)
            acc = jnp.zeros((n_sub, SUB, HG_DIM), F32)
            for s in range(SUB):
                d = b3 - b3[:, s:s + 1, :]
                e = jnp.exp(jnp.where(tio >= s, d, -jnp.inf))
                w = jnp.sum(q3 * e * k3[:, s:s + 1, :], axis=-1, keepdims=True)
                acc = acc + w * v3[:, s:s + 1, :]
            o = o + acc.reshape(CHUNK, HG_DIM)
            parts = [jnp.zeros((SUB, HG_DIM), F32)]
            for i in range(1, n_sub):
                rho = bh[i * SUB - 1:i * SUB, :]
                qi = qh[i * SUB:(i + 1) * SUB, :] * jnp.exp(bh[i * SUB:(i + 1) * SUB, :] - rho)
                ki = kh * jnp.exp(jnp.where(rio < i * SUB, rho - bh, -jnp.inf))
                a = _dot_nt(qi.astype(BF16), ki.astype(BF16))
                parts.append(_dot(a.astype(BF16), vb))
            o = o + jnp.concatenate(parts, axis=0)
            st_ref[h] = st * e_last[:, hs] + _dot_tn(vb, k_end[:, hs].astype(BF16))
            o = o * lax.rsqrt(jnp.mean(o * o, axis=-1, keepdims=True) + RMS_EPS) * ng[:, hs]
            gh = g[:, hs]
            o_ref[rows, hs] = (o * (gh * _sigmoid(gh))).astype(o_ref.dtype)
        return carry

    lax.fori_loop(0, n_chunks, chunk_body, 0)


def _hgrn2(proj, lb, norm_g, bsz, seq, dg, tb):
    n_heads = dg // HG_DIM
    tb = min(tb, seq)
    nt = seq // tb
    lbf = lb.astype(F32).reshape(1, dg)
    la = jnp.log(jnp.maximum(lbf, 0.0))
    lc = jnp.log1p(-lbf)
    oml = 1.0 - lbf
    r = jnp.arange(CHUNK)
    tri = (r[:, None] >= r[None, :]).astype(BF16)
    col = lambda j: pl.BlockSpec((tb, dg), lambda b, t, j=j: (b * nt + t, j))
    par = pl.BlockSpec((1, dg), lambda b, t: (0, 0))
    return pl.pallas_call(
        functools.partial(_hgrn2_kernel, n_heads=n_heads, n_chunks=tb // CHUNK),
        grid=(bsz, nt),
        in_specs=[col(0), col(1), col(2), col(3), par, par, par, par,
                  pl.BlockSpec((CHUNK, CHUNK), lambda b, t: (0, 0))],
        out_specs=pl.BlockSpec((tb, dg), lambda b, t: (b * nt + t, 0)),
        out_shape=jax.ShapeDtypeStruct((bsz * seq, dg), BF16),
        scratch_shapes=[pltpu.VMEM((n_heads, HG_DIM, HG_DIM), F32)],
        compiler_params=_params(("parallel", "arbitrary")),
        name="hgrn2",
    )(proj, proj, proj, proj, la, lc, oml, norm_g.astype(F32).reshape(1, dg), tri)


def _s5_kernel(u_ref, bbar_ref, lam_ref, cmat_ref, dskip_ref, gw_ref, gb_ref, o_ref, xs_ref, h_ref, *, tb, bsz, ns, lane_chunk):
    @pl.when(pl.program_id(0) == 0)
    def _():
        h_ref[...] = jnp.zeros_like(h_ref)

    u = u_ref[...]
    xs_ref[...] = _dot(u.astype(BF16), bbar_ref[...])
    for c in range(ns // lane_chunk):
        re = slice(c * lane_chunk, (c + 1) * lane_chunk)
        im = slice(ns + c * lane_chunk, ns + (c + 1) * lane_chunk)
        lr = jnp.broadcast_to(lam_ref[0:1, re], (bsz, lane_chunk))
        li = jnp.broadcast_to(lam_ref[1:2, re], (bsz, lane_chunk))

        def step(t, carry):
            hr, hi = carry
            rows = pl.ds(pl.multiple_of(t * bsz, bsz), bsz)
            nr = lr * hr - li * hi + xs_ref[rows, re]
            ni = lr * hi + li * hr + xs_ref[rows, im]
            xs_ref[rows, re] = nr
            xs_ref[rows, im] = ni
            return nr, ni

        hr, hi = lax.fori_loop(0, tb, step, (h_ref[:, re], h_ref[:, im]), unroll=4)
        h_ref[:, re] = hr
        h_ref[:, im] = hi
    y = _dot(xs_ref[...].astype(BF16), cmat_ref[...]) + dskip_ref[...] * u
    y = 0.5 * y * (1.0 + jnp.tanh(math.sqrt(2.0 / math.pi) * (y + 0.044715 * (y * y * y))))
    gate = _sigmoid(_dot(y.astype(BF16), gw_ref[...]) + gb_ref[...])
    o_ref[...] = (y * gate).astype(o_ref.dtype)


def _s5(u_tm, a_re, a_im, log_dt, b_re, b_im, c_re, c_im, d_skip, glu_w, glu_b, bsz, tb):
    dg = u_tm.shape[1]
    seq = u_tm.shape[0] // bsz
    groups, p = a_re.shape
    ch = dg // groups
    ns = groups * p
    dt = jnp.exp(log_dt.astype(F32))[:, None]
    ar, ai = a_re.astype(F32), a_im.astype(F32)
    mag = jnp.exp(ar * dt)
    lr, li = mag * jnp.cos(ai * dt), mag * jnp.sin(ai * dt)
    den = ar * ar + ai * ai
    zr = ((lr - 1.0) * ar + li * ai) / den
    zi = (li * ar - (lr - 1.0) * ai) / den
    br, bi = b_re.astype(F32), b_im.astype(F32)
    bbar_r = zr[..., None] * br - zi[..., None] * bi
    bbar_i = zr[..., None] * bi + zi[..., None] * br
    eye = jnp.eye(groups, dtype=F32)

    def blockdiag_in(w):
        return jnp.einsum('gpc,gh->gchp', w, eye).reshape(groups * ch, groups * p)

    def blockdiag_out(w):
        return jnp.einsum('gcp,gh->gphc', w, eye).reshape(groups * p, groups * ch)

    bbar = jnp.concatenate([blockdiag_in(bbar_r), blockdiag_in(bbar_i)], axis=1).astype(BF16)
    cmat = jnp.concatenate([blockdiag_out(c_re.astype(F32)), -blockdiag_out(c_im.astype(F32))], axis=0).astype(BF16)
    lam = jnp.stack([lr.reshape(ns), li.reshape(ns)], axis=0)
    tb = min(tb, seq)
    full = lambda a: pl.BlockSpec(a.shape, lambda t: (0,) * a.ndim)
    dsk = d_skip.astype(F32).reshape(1, dg)
    gw = glu_w.astype(BF16)
    gb = glu_b.astype(F32).reshape(1, dg)
    return pl.pallas_call(
        functools.partial(_s5_kernel, tb=tb, bsz=bsz, ns=ns, lane_chunk=512),
        grid=(seq // tb,),
        in_specs=[pl.BlockSpec((tb * bsz, dg), lambda t: (t, 0)), full(bbar), full(lam), full(cmat), full(dsk),
                  full(gw), full(gb)],
        out_specs=pl.BlockSpec((tb * bsz, dg), lambda t: (t, 0)),
        out_shape=jax.ShapeDtypeStruct((seq * bsz, dg), BF16),
        scratch_shapes=[pltpu.VMEM((tb * bsz, 2 * ns), F32), pltpu.VMEM((bsz, 2 * ns), F32)],
        compiler_params=_params(("arbitrary",)),
        name="s5",
    )(u_tm, bbar, lam, cmat, dsk, gw, gb)


def _sb_kernel(q_ref, k_ref, v_ref, tri_ref, cmr_ref, o_ref, qb_ref, kb_ref, vb_ref, z0_ref, z1_ref, pre0_ref,
               pre1_ref, tot0_ref, tot1_ref, *, tq, nq, scale):
    qb_ref[...] = (q_ref[...] * scale).astype(BF16)
    kb_ref[...] = k_ref[...].astype(BF16)
    vb_ref[...] = v_ref[...].astype(BF16)
    z_refs, pre_refs, tot_refs = (z0_ref, z1_ref), (pre0_ref, pre1_ref), (tot0_ref, tot1_ref)
    for ref in z_refs + pre_refs + tot_refs:
        ref[...] = jnp.zeros_like(ref)
    tri = tri_ref[...]
    cmr = cmr_ref[...]
    n_steps = nq * (nq + 1) // 2 + 2

    def rows(blk):
        return pl.ds(pl.multiple_of(blk * tq, tq), tq)

    def step(c, slot):
        qa, ia, qb, ib, qc, ic, carry, acc = c
        qa_c = jnp.minimum(qa, nq - 1)
        z_refs[slot][...] = _dot_nt(qb_ref[rows(qa_c), :], kb_ref[rows(jnp.maximum(qa_c - ia, 0)), :])
        below = cmr < ib * tq
        z = jnp.where(below, z_refs[1 - slot][...], -jnp.inf)
        nz = -z
        log_rest = jnp.minimum(nz, 0.0) - jnp.log2(1.0 + jnp.exp2(jnp.minimum(z, nz)))
        hi = log_rest.astype(BF16)
        lo = (log_rest - hi.astype(F32)).astype(BF16)
        after = _dot(hi, tri) + _dot(lo, tri)
        pre_refs[1 - slot][...] = (z + log_rest) + after
        tot_refs[1 - slot][...] = jnp.broadcast_to(jnp.sum(log_rest, axis=-1, keepdims=True), (tq, SB_DIM))
        first = ic == 0
        carry = jnp.where(first, 0.0, carry)
        acc = jnp.where(first, 0.0, acc)
        w = jnp.exp2(pre_refs[slot][...] + jnp.concatenate([carry] * (tq // SB_DIM), axis=1))
        acc = acc + _dot(w.astype(BF16), vb_ref[rows(jnp.maximum(qc - ic, 0)), :])
        o_ref[rows(qc), :] = acc.astype(o_ref.dtype)
        wrap = ia >= qa
        return (jnp.where(wrap, qa + 1, qa), jnp.where(wrap, 0, ia + 1), qa_c, ia, qb, ib,
                carry + tot_refs[slot][...], acc)

    zero = jnp.int32(0)
    c = (zero, zero, zero, zero, zero, zero, jnp.zeros((tq, SB_DIM), F32), jnp.zeros((tq, SB_DIM), F32))
    first_slot = 0
    if n_steps % 2:
        c = step(c, 0)
        first_slot = 1
    lax.fori_loop(0, n_steps // 2, lambda _, c: step(step(c, first_slot), 1 - first_slot), c)


def _stick_breaking(proj, bsz, seq, dg, col0, tq):
    n_heads = dg // SB_DIM
    tq = min(tq, seq)
    nq = seq // tq
    r = jnp.arange(tq, dtype=jnp.int32)
    tri = (r[:, None] > r[None, :]).astype(BF16)
    cmr = r[None, :] - r[:, None]
    cb = col0 // SB_DIM
    seq_blk = lambda j: pl.BlockSpec((seq, SB_DIM), lambda b, h, j=j: (b, cb + j * n_heads + h))
    const = lambda a: pl.BlockSpec(a.shape, lambda b, h: (0, 0))
    return pl.pallas_call(
        functools.partial(_sb_kernel, tq=tq, nq=nq, scale=SB_DIM ** -0.5 * math.log2(math.e)),
        grid=(bsz, n_heads),
        in_specs=[seq_blk(0), seq_blk(1), seq_blk(2), const(tri), const(cmr)],
        out_specs=pl.BlockSpec((seq, SB_DIM), lambda b, h: (b, h)),
        out_shape=jax.ShapeDtypeStruct((bsz * seq, dg), BF16),
        scratch_shapes=[pltpu.VMEM((seq, SB_DIM), BF16), pltpu.VMEM((seq, SB_DIM), BF16),
                        pltpu.VMEM((seq, SB_DIM), BF16)] + [pltpu.VMEM((tq, tq), F32)] * 4
                       + [pltpu.VMEM((tq, SB_DIM), F32)] * 2,
        compiler_params=_params(("parallel", "parallel")),
        name="stick_breaking",
    )(proj, proj, proj, tri, cmr)


def _rwkv7_kernel(r_ref, k_ref, v_ref, tail_ref, mu_r_ref, mu_k_ref, mu_v_ref, mu_t_ref, w0_ref, w2_ref, a0_ref,
                  a2_ref, g2_ref, kk_ref, ka_ref, rk_ref, gng_ref, gnb_ref, tri_ref, o_ref,
                  mt_ref, prev_ref, prev_t_ref, *, n_pairs):
    @pl.when(pl.program_id(1) == 0)
    def _():
        mt_ref[...] = jnp.zeros_like(mt_ref)
        prev_ref[...] = jnp.zeros_like(prev_ref)
        prev_t_ref[...] = jnp.zeros_like(prev_t_ref)

    def shifted(p, prev_row, mu):
        rolled = pltpu.roll(p, 1, axis=0)
        first = lax.broadcasted_iota(jnp.int32, p.shape, 0) == 0
        prev = jnp.where(first, prev_row, rolled)
        return p + mu * (prev - p)

    pr, pk, pv, pt = r_ref[...], k_ref[...], v_ref[...], tail_ref[...]
    r = shifted(pr, prev_ref[0:1, :], mu_r_ref[...])
    k = shifted(pk, prev_ref[1:2, :], mu_k_ref[...])
    v = shifted(pv, prev_ref[2:3, :], mu_v_ref[...])
    tail = shifted(pt, prev_t_ref[...], mu_t_ref[...])
    prev_ref[0:1, :] = pr[CHUNK - 1:CHUNK, :]
    prev_ref[1:2, :] = pk[CHUNK - 1:CHUNK, :]
    prev_ref[2:3, :] = pv[CHUNK - 1:CHUNK, :]
    prev_t_ref[...] = pt[CHUNK - 1:CHUNK, :]

    xw = tail[:, 0:RW_W_RANK]
    xa = tail[:, RW_W_RANK:RW_W_RANK + RW_A_RANK]
    xg = tail[:, RW_W_RANK + RW_A_RANK:]
    w_log = -_softplus(-(w0_ref[...] + _dot(jnp.tanh(xw).astype(BF16), w2_ref[...]))) - 0.5
    logw = -jnp.exp(w_log)
    a = _sigmoid(a0_ref[...] + _dot(xa.astype(BF16), a2_ref[...]))
    g = _dot(_sigmoid(xg).astype(BF16), g2_ref[...])
    lane = lax.broadcasted_iota(jnp.int32, (CHUNK, 2 * RW_DIM), 1)
    head0 = lane < RW_DIM
    pairs = [slice(p * 2 * RW_DIM, (p + 1) * 2 * RW_DIM) for p in range(n_pairs)]

    def head_sum(x):
        s0 = jnp.sum(jnp.where(head0, x, 0.0), axis=-1, keepdims=True)
        s1 = jnp.sum(jnp.where(head0, 0.0, x), axis=-1, keepdims=True)
        return jnp.where(head0, s0, s1)

    def head_sums(x):
        return jnp.concatenate([head_sum(x[:, ps]) for ps in pairs], axis=1)

    kk = k * kk_ref[...]
    kk = kk * lax.rsqrt(jnp.maximum(head_sums(kk * kk), 1e-24))
    k2 = k * (1.0 + (a - 1.0) * ka_ref[...])
    beta = kk * a
    lp = _dot_exact_lhs(tri_ref[...], logw)
    lp_end = lp[CHUNK - 1:CHUNK, :]
    e_pos = jnp.exp(lp)
    e_neg = jnp.exp(-lp)
    e_end = jnp.exp(lp_end - lp)
    r_b = r * e_pos
    a_b = kk * jnp.exp(lp - logw)
    k_b = k2 * e_neg
    b_b = beta * e_neg
    k_t = k2 * e_end
    b_t = beta * e_end
    bonus = head_sums(r * k2 * rk_ref[...]) * v

    n2 = 2 * CHUNK
    ri = lax.broadcasted_iota(jnp.int32, (n2, n2), 0)
    ci = lax.broadcasted_iota(jnp.int32, (n2, n2), 1)
    same = (ri < CHUNK) == (ci < CHUNK)
    strict = jnp.logical_and(same, ci < ri)
    incl = jnp.logical_and(same, ci <= ri)
    eye = (ri == ci).astype(F32)

    def stack(x):
        return jnp.concatenate([jnp.where(head0, x, 0.0), jnp.where(head0, 0.0, x)], axis=0)

    def stack2(x, y):
        return jnp.concatenate([stack(x), stack(y)], axis=0).astype(BF16)

    mt = [mt_ref[p] for p in range(n_pairs)]
    ar = [stack2(a_b[:, ps], r_b[:, ps]) for ps in pairs]
    kb = [stack2(k_b[:, ps], b_b[:, ps]) for ps in pairs]
    v_s = [stack(v[:, ps]).astype(BF16) for ps in pairs]
    gg = [_dot_nt(x_, y_) for x_, y_ in zip(ar, kb)]
    am = [_dot_nt(x_, m_.astype(BF16)) for x_, m_ in zip(ar, mt)]
    lg = [jnp.concatenate([jnp.where(strict, g_[:n2, :n2], 0.0), jnp.where(incl, g_[n2:, :n2], 0.0)],
                          axis=0).astype(BF16) for g_ in gg]
    lgv = [_dot(x_, y_) for x_, y_ in zip(lg, v_s)]
    rhs = [(x_[:n2] + y_[:n2]).astype(BF16) for x_, y_ in zip(am, lgv)]
    x = [jnp.where(strict, -g_[:n2, n2:], 0.0) for g_ in gg]
    t_inv = [eye + x_ for x_ in x]
    for _ in range(int(math.log2(CHUNK)) - 1):
        xb = [x_.astype(BF16) for x_ in x]
        x = [_dot(x_, x_) for x_ in xb]
        t_inv = [t_ + _dot(t_.astype(BF16), x_.astype(BF16)) for t_, x_ in zip(t_inv, x)]
    u = [_dot(t_.astype(BF16), r_) for t_, r_ in zip(t_inv, rhs)]
    gb = [jnp.where(incl, g_[n2:, n2:], 0.0).astype(BF16) for g_ in gg]
    o_s = [x_[n2:] + y_[n2:] - _dot(g_, u_.astype(BF16)) for x_, y_, g_, u_ in zip(am, lgv, gb, u)]
    vu = [jnp.concatenate([v_, (-u_).astype(BF16)], axis=0) for v_, u_ in zip(v_s, u)]
    kbt = [stack2(k_t[:, ps], b_t[:, ps]) for ps in pairs]
    for p, ps in enumerate(pairs):
        mt_ref[p] = mt[p] * e_pos[CHUNK - 1:CHUNK, ps] + _dot_tn(vu[p], kbt[p])
        out = o_s[p][:CHUNK, :] + o_s[p][CHUNK:, :]
        d = out - head_sum(out) * (1.0 / RW_DIM)
        var = head_sum(d * d) * (1.0 / RW_DIM)
        out = d * lax.rsqrt(var + RW_GN_EPS) * gng_ref[:, ps] + gnb_ref[:, ps]
        o_ref[:, ps] = ((out + bonus[:, ps]) * g[:, ps]).astype(o_ref.dtype)


def _rwkv7(proj, mu, w0, w2, a0, a2, g2, k_k, k_a, r_k, gn_g, gn_b, bsz, seq, dg, col0):
    n_heads = dg // RW_DIM
    n_pairs = n_heads // 2
    nt = seq // CHUNK
    tail_w = RW_W_RANK + RW_A_RANK + RW_G_RANK
    cb = col0 // dg
    tb_col = (col0 + 3 * dg) // tail_w
    muf = mu.astype(F32)
    vec = lambda a: a.astype(F32).reshape(1, -1)
    rr = jnp.arange(CHUNK)
    tri = (rr[:, None] >= rr[None, :]).astype(BF16)
    rowblk = lambda j: pl.BlockSpec((CHUNK, dg), lambda b, t, j=j: (b * nt + t, cb + j))
    full = lambda a: pl.BlockSpec(a.shape, lambda b, t: (0,) * a.ndim)
    args = [vec(muf[0:dg]), vec(muf[dg:2 * dg]), vec(muf[2 * dg:3 * dg]), vec(muf[3 * dg:]),
            vec(w0), w2.astype(BF16), vec(a0), a2.astype(BF16), g2.astype(BF16), vec(k_k), vec(k_a), vec(r_k),
            vec(gn_g), vec(gn_b), tri]
    return pl.pallas_call(
        functools.partial(_rwkv7_kernel, n_pairs=n_pairs),
        grid=(bsz, nt),
        in_specs=[rowblk(0), rowblk(1), rowblk(2),
                  pl.BlockSpec((CHUNK, tail_w), lambda b, t: (b * nt + t, tb_col))] + [full(a) for a in args],
        out_specs=pl.BlockSpec((CHUNK, dg), lambda b, t: (b * nt + t, 0)),
        out_shape=jax.ShapeDtypeStruct((bsz * seq, dg), BF16),
        scratch_shapes=[pltpu.VMEM((n_pairs, 2 * RW_DIM, 2 * RW_DIM), F32), pltpu.VMEM((8, dg), F32),
                        pltpu.VMEM((1, tail_w), F32)],
        compiler_params=_params(("parallel", "arbitrary")),
        name="rwkv7",
    )(proj, proj, proj, proj, *args)


def kernel(x, c, ada_w, ada_b, w_in, w_out, hg_lb_logits, hg_norm_g, s5_a_re, s5_a_im, s5_log_dt, s5_b_re, s5_b_im, s5_c_re, s5_c_im, s5_d, s5_glu_w, s5_glu_b, rw_mu, rw_w0, rw_w2, rw_a0, rw_a2, rw_g2, rw_k_k, rw_k_a, rw_r_k, rw_gn_g, rw_gn_b, ln1_g, ln1_b, ffn_w1, ffn_w3, ffn_w2, ln2_g, ln2_b):
    bsz, seq, d = x.shape
    depth = w_in.shape[0]
    n_in = w_in.shape[2]
    dg = d // N_MIXERS
    d_ff = ffn_w1.shape[2]
    alpha = (2 * depth) ** 0.25
    tokens = bsz * seq

    lb_all = jnp.cumsum(jax.nn.softmax(hg_lb_logits.astype(F32), axis=0), axis=0)
    lb_all = lb_all - lb_all[:1]

    mod = _ada_mod(c.astype(F32), ada_w, ada_b)
    mod = mod.reshape(depth, bsz, 6, 1, d)
    shift1, scale1, gate1, shift2, scale2, gate2 = [mod[:, :, i] for i in range(6)]
    zeros_vec = jnp.zeros((bsz, 1, d), F32)

    n_pad = -(-n_in // 1024) * 1024
    x2 = x.reshape(tokens, d)
    h = _modulate(x2, scale1[0], shift1[0], seq)
    for l in range(depth):
        w_in_l = jnp.pad(w_in[l].astype(BF16), ((0, 0), (0, n_pad - n_in)))
        proj = _matmul(h, w_in_l, F32, 1024, 1024)
        o_a = _hgrn2(proj, lb_all[l], hg_norm_g[l], bsz, seq, dg, 256)
        u_tm = proj.reshape(bsz, seq, n_pad)[:, :, 4 * dg:5 * dg].transpose(1, 0, 2).reshape(tokens, dg)
        o_b = _s5(u_tm, s5_a_re[l], s5_a_im[l], s5_log_dt[l], s5_b_re[l], s5_b_im[l], s5_c_re[l], s5_c_im[l],
                  s5_d[l], s5_glu_w[l], s5_glu_b[l], bsz, 64)
        o_b = o_b.reshape(seq, bsz, dg).transpose(1, 0, 2).reshape(tokens, dg)
        o_c = _stick_breaking(proj, bsz, seq, dg, 5 * dg, 256)
        o_d = _rwkv7(proj, rw_mu[l], rw_w0[l], rw_w2[l], rw_a0[l], rw_a2[l], rw_g2[l], rw_k_k[l], rw_k_a[l],
                     rw_r_k[l].reshape(-1), rw_gn_g[l], rw_gn_b[l], bsz, seq, dg, 8 * dg)
        x2, h = _out_proj_norm([o_a, o_b, o_c, o_d], w_out[l].astype(BF16), x2, gate1[l],
                               ln1_g[l].astype(F32).reshape(1, d), ln1_b[l].astype(F32).reshape(1, d),
                               scale2[l], shift2[l], seq, alpha, 256)
        u = _ffn_up(h, ffn_w1[l].astype(BF16), ffn_w3[l].astype(BF16), 1024, 512)
        last = l == depth - 1
        x2, h = _ffn_down_norm(u, ffn_w2[l].astype(BF16), x2, gate2[l],
                               ln2_g[l].astype(F32).reshape(1, d), ln2_b[l].astype(F32).reshape(1, d),
                               zeros_vec if last else scale1[l + 1], zeros_vec if last else shift1[l + 1],
                               seq, alpha, 256)
    return x2.reshape(bsz, seq, d)
```

```python
import functools
import math

import jax
import jax.numpy as jnp
from jax import lax
from jax.experimental import pallas as pl
from jax.experimental.pallas import tpu as pltpu

F32 = jnp.float32
BF16 = jnp.bfloat16

N_MIXERS = 4
CHUNK = 64
SUB = 16
HG_DIM = 128
S5_CH = 16
S5_STATE = 64
SB_DIM = 128
RW_DIM = 64
RW_W_RANK = 64
RW_A_RANK = 64
RW_G_RANK = 128
RW_GN_EPS = 64e-5
LN_EPS = 1e-5
RMS_EPS = 1e-6
LOG2E = math.log2(math.e)
V7X_VMEM_LIMIT = 56 * 1024 * 1024


def _params(sem, vmem=V7X_VMEM_LIMIT):
    return pltpu.CompilerParams(dimension_semantics=sem, vmem_limit_bytes=vmem)


def _dot(a, b):
    return jnp.dot(a, b, preferred_element_type=F32)


def _dot_nt(a, b):
    return lax.dot_general(a, b, (((1,), (1,)), ((), ())), preferred_element_type=F32)


def _dot_tn(a, b):
    return lax.dot_general(a, b, (((0,), (0,)), ((), ())), preferred_element_type=F32)


def _split3(x):
    h = x.astype(BF16)
    r = x - h.astype(F32)
    m = r.astype(BF16)
    lo = (r - m.astype(F32)).astype(BF16)
    return h, m, lo


def _dot_exact_lhs(a_bf16, x):
    h, m, lo = _split3(x)
    return _dot(a_bf16, h) + _dot(a_bf16, m) + _dot(a_bf16, lo)


def _dot_exact_rhs(x, b_bf16):
    h, m, lo = _split3(x)
    return _dot(h, b_bf16) + _dot(m, b_bf16) + _dot(lo, b_bf16)


def _softplus(x):
    return jnp.maximum(x, 0.0) + jnp.log1p(jnp.exp(-jnp.abs(x)))


def _sigmoid(x):
    return jax.nn.sigmoid(x)


def _layer_norm_rows(y, g, b):
    mu = jnp.mean(y, axis=-1, keepdims=True)
    d = y - mu
    var = jnp.mean(d * d, axis=-1, keepdims=True)
    return d * lax.rsqrt(var + LN_EPS) * g + b


def _ada_kernel(c_ref, w_ref, b_ref, o_ref):
    c = c_ref[...]
    ca = (c * _sigmoid(c)).astype(BF16)
    o_ref[0] = _dot(ca, w_ref[0].astype(BF16)) + b_ref[0]


def _ada_mod(c, ada_w, ada_b):
    depth, d, n = ada_w.shape
    bsz = c.shape[0]
    tn = 1024
    return pl.pallas_call(
        _ada_kernel,
        grid=(depth, n // tn),
        in_specs=[pl.BlockSpec((bsz, d), lambda l, j: (0, 0)),
                  pl.BlockSpec((1, d, tn), lambda l, j: (l, 0, j)),
                  pl.BlockSpec((1, 1, tn), lambda l, j: (l, 0, j))],
        out_specs=pl.BlockSpec((1, bsz, tn), lambda l, j: (l, 0, j)),
        out_shape=jax.ShapeDtypeStruct((depth, bsz, n), F32),
        compiler_params=_params(("parallel", "parallel")),
        name="ada_mod",
    )(c, ada_w, ada_b.reshape(depth, 1, n))


def _modulate_kernel(x_ref, sc_ref, sh_ref, o_ref):
    o_ref[...] = (x_ref[...] * (1.0 + sc_ref[0]) + sh_ref[0]).astype(o_ref.dtype)


def _modulate(x2, scale, shift, seq):
    t, d = x2.shape
    tm = min(1024, seq)
    per = seq // tm
    vec = pl.BlockSpec((1, 1, d), lambda i: (i // per, 0, 0))
    return pl.pallas_call(
        _modulate_kernel,
        grid=(t // tm,),
        in_specs=[pl.BlockSpec((tm, d), lambda i: (i, 0)), vec, vec],
        out_specs=pl.BlockSpec((tm, d), lambda i: (i, 0)),
        out_shape=jax.ShapeDtypeStruct((t, d), BF16),
        compiler_params=_params(("parallel",)),
        name="modulate",
    )(x2, scale, shift)


def _in_proj_kernel(a_ref, w_ref, o_ref, u_ref, *, j_u, off_u):
    acc = _dot(a_ref[...], w_ref[...])
    o_ref[...] = acc

    @pl.when(pl.program_id(1) == j_u)
    def _():
        u_ref[...] = acc[:, off_u:off_u + u_ref.shape[1]]


def _in_proj(a, w, seq, col_u, dg, tm, tn):
    m, k = a.shape
    n = w.shape[1]
    tm = min(tm, seq)
    per = seq // tm
    j_u, off_u = col_u // tn, col_u % tn
    assert off_u + dg <= tn
    return pl.pallas_call(
        functools.partial(_in_proj_kernel, j_u=j_u, off_u=off_u),
        grid=(m // tm, n // tn),
        in_specs=[pl.BlockSpec((tm, k), lambda i, j: (i, 0)),
                  pl.BlockSpec((k, tn), lambda i, j: (0, j))],
        out_specs=[pl.BlockSpec((tm, tn), lambda i, j: (i, j)),
                   pl.BlockSpec((tm, dg), lambda i, j: (i % per, i // per))],
        out_shape=[jax.ShapeDtypeStruct((m, n), F32), jax.ShapeDtypeStruct((seq, (m // seq) * dg), F32)],
        compiler_params=_params(("parallel", "arbitrary")),
        name="in_proj",
    )(a, w)


def _ffn_up_kernel(a_ref, w1_ref, w3_ref, o_ref):
    a = a_ref[...]
    p1 = _dot(a, w1_ref[...])
    p3 = _dot(a, w3_ref[...])
    o_ref[...] = (p1 * _sigmoid(p1) * p3).astype(o_ref.dtype)


def _ffn_up(a, w1, w3, tm, tn):
    m, k = a.shape
    n = w1.shape[1]
    tm = min(tm, m)
    wspec = pl.BlockSpec((k, tn), lambda i, j: (0, j))
    return pl.pallas_call(
        _ffn_up_kernel,
        grid=(m // tm, n // tn),
        in_specs=[pl.BlockSpec((tm, k), lambda i, j: (i, 0)), wspec, wspec],
        out_specs=pl.BlockSpec((tm, tn), lambda i, j: (i, j)),
        out_shape=jax.ShapeDtypeStruct((m, n), BF16),
        compiler_params=_params(("parallel", "parallel")),
        name="ffn_up",
    )(a, w1, w3)


def _residual_norm(acc, x_ref, gate_ref, lng_ref, lnb_ref, sc_ref, sh_ref, xo_ref, ho_ref, alpha):
    y = alpha * x_ref[...] + gate_ref[0] * acc
    xn = _layer_norm_rows(y, lng_ref[...], lnb_ref[...])
    xo_ref[...] = xn
    ho_ref[...] = (xn * (1.0 + sc_ref[0]) + sh_ref[0]).astype(ho_ref.dtype)


def _out_proj_kernel(oa_ref, ob_ref, oc_ref, od_ref, w_ref, x_ref, gate_ref, lng_ref, lnb_ref, sc_ref, sh_ref,
                     xo_ref, ho_ref, *, alpha, dg):
    acc = _dot(oa_ref[...], w_ref[0 * dg:1 * dg, :])
    acc += _dot(ob_ref[...], w_ref[1 * dg:2 * dg, :])
    acc += _dot(oc_ref[...], w_ref[2 * dg:3 * dg, :])
    acc += _dot(od_ref[...], w_ref[3 * dg:4 * dg, :])
    _residual_norm(acc, x_ref, gate_ref, lng_ref, lnb_ref, sc_ref, sh_ref, xo_ref, ho_ref, alpha)


def _out_proj_norm(outs, w, x2, gate, lng, lnb, scale, shift, seq, alpha, tm):
    t, d = x2.shape
    dg = outs[0].shape[1]
    tm = min(tm, seq)
    per = seq // tm
    row = lambda i: (i, 0)
    vec = pl.BlockSpec((1, 1, d), lambda i: (i // per, 0, 0))
    par = pl.BlockSpec((1, d), lambda i: (0, 0))
    return pl.pallas_call(
        functools.partial(_out_proj_kernel, alpha=alpha, dg=dg),
        grid=(t // tm,),
        in_specs=[pl.BlockSpec((tm, dg), row), pl.BlockSpec((tm, dg), lambda i: (i % per, i // per)),
                  pl.BlockSpec((tm, dg), row), pl.BlockSpec((tm, dg), row)] + [
            pl.BlockSpec(w.shape, lambda i: (0, 0), pipeline_mode=pl.Buffered(1)),
            pl.BlockSpec((tm, d), row), vec, par, par, vec, vec],
        out_specs=[pl.BlockSpec((tm, d), row), pl.BlockSpec((tm, d), row)],
        out_shape=[jax.ShapeDtypeStruct((t, d), F32), jax.ShapeDtypeStruct((t, d), BF16)],
        compiler_params=_params(("parallel",)),
        name="out_proj_norm",
    )(*outs, w, x2, gate, lng, lnb, scale, shift)


def _ffn_down_kernel(u_ref, w_ref, x_ref, gate_ref, lng_ref, lnb_ref, sc_ref, sh_ref, xo_ref, ho_ref, *, alpha):
    acc = _dot(u_ref[...], w_ref[...])
    _residual_norm(acc, x_ref, gate_ref, lng_ref, lnb_ref, sc_ref, sh_ref, xo_ref, ho_ref, alpha)


def _ffn_down_norm(u, w, x2, gate, lng, lnb, scale, shift, seq, alpha, tm):
    t, d = x2.shape
    kk = u.shape[1]
    tm = min(tm, seq)
    per = seq // tm
    row = lambda i: (i, 0)
    vec = pl.BlockSpec((1, 1, d), lambda i: (i // per, 0, 0))
    par = pl.BlockSpec((1, d), lambda i: (0, 0))
    return pl.pallas_call(
        functools.partial(_ffn_down_kernel, alpha=alpha),
        grid=(t // tm,),
        in_specs=[pl.BlockSpec((tm, kk), row),
                  pl.BlockSpec((kk, d), lambda i: (0, 0), pipeline_mode=pl.Buffered(1)),
                  pl.BlockSpec((tm, d), row), vec, par, par, vec, vec],
        out_specs=[pl.BlockSpec((tm, d), row), pl.BlockSpec((tm, d), row)],
        out_shape=[jax.ShapeDtypeStruct((t, d), F32), jax.ShapeDtypeStruct((t, d), BF16)],
        compiler_params=_params(("parallel",)),
        name="ffn_down_norm",
    )(u, w, x2, gate, lng, lnb, scale, shift)


def _hgrn2_kernel(q_ref, f_ref, i_ref, g_ref, la_ref, lc_ref, oml_ref, ng_ref, tri_ref, o_ref, st_ref, *, n_heads, n_chunks):
    @pl.when(pl.program_id(1) == 0)
    def _():
        st_ref[...] = jnp.zeros_like(st_ref)

    la = la_ref[...]
    lc = lc_ref[...]
    oml = oml_ref[...]
    ng = ng_ref[...]
    tri = tri_ref[...]
    n_sub = CHUNK // SUB
    tio = lax.broadcasted_iota(jnp.int32, (n_sub, SUB, HG_DIM), 1)
    rio = lax.broadcasted_iota(jnp.int32, (CHUNK, n_heads * HG_DIM), 0)

    def chunk_body(c, carry):
        rows = pl.ds(pl.multiple_of(c * CHUNK, CHUNK), CHUNK)
        z = f_ref[rows, :]
        log_sig = jnp.minimum(z, 0.0) - jnp.log1p(jnp.exp(-jnp.abs(z)))
        y = lc + log_sig
        mx = jnp.maximum(la, y)
        log_f = mx + jnp.log1p(jnp.exp(-jnp.abs(la - y)))
        kg = oml * _sigmoid(-z)
        q = q_ref[rows, :]
        qf = q * _sigmoid(q)
        v = i_ref[rows, :]
        g = g_ref[rows, :]
        b = _dot_exact_lhs(tri, log_f) * LOG2E
        b_last = b[CHUNK - 1:CHUNK, :]
        e_last = jnp.exp2(b_last)
        k_end = (kg * jnp.exp2(b_last - b)).astype(BF16)
        q_in = (qf * jnp.exp2(b)).astype(BF16)
        vb = v.astype(BF16)
        heads = [slice(h * HG_DIM, (h + 1) * HG_DIM) for h in range(n_heads)]
        st = [st_ref[h] for h in range(n_heads)]
        o = [_dot_nt(q_in[:, hs], st_.astype(BF16)) for hs, st_ in zip(heads, st)]
        split = lambda x: [x[:, hs].reshape(n_sub, SUB, HG_DIM) for hs in heads]
        b3, q3, k3, v3 = split(b), split(qf), split(kg), split(v)
        acc = [jnp.zeros((n_sub, SUB, HG_DIM), F32) for _ in heads]
        for s in range(SUB):
            causal = tio >= s
            for h in range(n_heads):
                e = jnp.exp2(jnp.where(causal, b3[h] - b3[h][:, s:s + 1, :], -jnp.inf))
                w = jnp.sum(q3[h] * e * k3[h][:, s:s + 1, :], axis=-1, keepdims=True)
                acc[h] = acc[h] + w * v3[h][:, s:s + 1, :]
        parts = [[jnp.zeros((SUB, HG_DIM), F32)] for _ in heads]
        for i in range(1, n_sub):
            rho = b[i * SUB - 1:i * SUB, :]
            qi = (qf[i * SUB:(i + 1) * SUB, :] * jnp.exp2(b[i * SUB:(i + 1) * SUB, :] - rho)).astype(BF16)
            ki = (kg * jnp.exp2(jnp.where(rio < i * SUB, rho - b, -jnp.inf))).astype(BF16)
            a = [_dot_nt(qi[:, hs], ki[:, hs]).astype(BF16) for hs in heads]
            for h, hs in enumerate(heads):
                parts[h].append(_dot(a[h], vb[:, hs]))
        gs = g * _sigmoid(g)
        for h, hs in enumerate(heads):
            st_ref[h] = st[h] * e_last[:, hs] + _dot_tn(vb[:, hs], k_end[:, hs])
            oh = o[h] + acc[h].reshape(CHUNK, HG_DIM) + jnp.concatenate(parts[h], axis=0)
            oh = oh * lax.rsqrt(jnp.mean(oh * oh, axis=-1, keepdims=True) + RMS_EPS) * ng[:, hs]
            o_ref[rows, hs] = (oh * gs[:, hs]).astype(o_ref.dtype)
        return carry

    lax.fori_loop(0, n_chunks, chunk_body, 0)


def _hgrn2(proj, lb, norm_g, bsz, seq, dg, tb):
    n_heads = dg // HG_DIM
    tb = min(tb, seq)
    nt = seq // tb
    lbf = lb.astype(F32).reshape(1, dg)
    la = jnp.log(jnp.maximum(lbf, 0.0))
    lc = jnp.log1p(-lbf)
    oml = 1.0 - lbf
    r = jnp.arange(CHUNK)
    tri = (r[:, None] >= r[None, :]).astype(BF16)
    col = lambda j: pl.BlockSpec((tb, dg), lambda b, t, j=j: (b * nt + t, j))
    par = pl.BlockSpec((1, dg), lambda b, t: (0, 0))
    return pl.pallas_call(
        functools.partial(_hgrn2_kernel, n_heads=n_heads, n_chunks=tb // CHUNK),
        grid=(bsz, nt),
        in_specs=[col(0), col(1), col(2), col(3), par, par, par, par,
                  pl.BlockSpec((CHUNK, CHUNK), lambda b, t: (0, 0))],
        out_specs=pl.BlockSpec((tb, dg), lambda b, t: (b * nt + t, 0)),
        out_shape=jax.ShapeDtypeStruct((bsz * seq, dg), BF16),
        scratch_shapes=[pltpu.VMEM((n_heads, HG_DIM, HG_DIM), F32)],
        compiler_params=_params(("parallel", "arbitrary")),
        name="hgrn2",
    )(proj, proj, proj, proj, la, lc, oml, norm_g.astype(F32).reshape(1, dg), tri)


def _s5_kernel(u_ref, bbar_ref, lam_ref, cmat_ref, dskip_ref, gw_ref, gb_ref, o_ref, xs_ref, h_ref, *, tb, bsz, ns, lane_chunk):
    @pl.when(pl.program_id(0) == 0)
    def _():
        h_ref[...] = jnp.zeros_like(h_ref)

    u = u_ref[...]
    xs_ref[...] = _dot(u.astype(BF16), bbar_ref[...])
    for c in range(ns // lane_chunk):
        re = slice(c * lane_chunk, (c + 1) * lane_chunk)
        im = slice(ns + c * lane_chunk, ns + (c + 1) * lane_chunk)
        lr = jnp.broadcast_to(lam_ref[0:1, re], (bsz, lane_chunk))
        li = jnp.broadcast_to(lam_ref[1:2, re], (bsz, lane_chunk))

        def step(t, carry):
            hr, hi = carry
            rows = pl.ds(pl.multiple_of(t * bsz, bsz), bsz)
            nr = lr * hr - li * hi + xs_ref[rows, re]
            ni = lr * hi + li * hr + xs_ref[rows, im]
            xs_ref[rows, re] = nr
            xs_ref[rows, im] = ni
            return nr, ni

        hr, hi = lax.fori_loop(0, tb, step, (h_ref[:, re], h_ref[:, im]), unroll=4)
        h_ref[:, re] = hr
        h_ref[:, im] = hi
    y = _dot(xs_ref[...].astype(BF16), cmat_ref[...]) + dskip_ref[...] * u
    y = 0.5 * y * (1.0 + jnp.tanh(math.sqrt(2.0 / math.pi) * (y + 0.044715 * (y * y * y))))
    gate = _sigmoid(_dot(y.astype(BF16), gw_ref[...]) + gb_ref[...])
    o_ref[...] = (y * gate).astype(o_ref.dtype)


def _s5(u_tm, a_re, a_im, log_dt, b_re, b_im, c_re, c_im, d_skip, glu_w, glu_b, bsz, tb):
    dg = u_tm.shape[1]
    seq = u_tm.shape[0] // bsz
    groups, p = a_re.shape
    ch = dg // groups
    ns = groups * p
    dt = jnp.exp(log_dt.astype(F32))[:, None]
    ar, ai = a_re.astype(F32), a_im.astype(F32)
    mag = jnp.exp(ar * dt)
    lr, li = mag * jnp.cos(ai * dt), mag * jnp.sin(ai * dt)
    den = ar * ar + ai * ai
    zr = ((lr - 1.0) * ar + li * ai) / den
    zi = (li * ar - (lr - 1.0) * ai) / den
    br, bi = b_re.astype(F32), b_im.astype(F32)
    bbar_r = zr[..., None] * br - zi[..., None] * bi
    bbar_i = zr[..., None] * bi + zi[..., None] * br
    eye = jnp.eye(groups, dtype=F32)

    def blockdiag_in(w):
        return jnp.einsum('gpc,gh->gchp', w, eye).reshape(groups * ch, groups * p)

    def blockdiag_out(w):
        return jnp.einsum('gcp,gh->gphc', w, eye).reshape(groups * p, groups * ch)

    bbar = jnp.concatenate([blockdiag_in(bbar_r), blockdiag_in(bbar_i)], axis=1).astype(BF16)
    cmat = jnp.concatenate([blockdiag_out(c_re.astype(F32)), -blockdiag_out(c_im.astype(F32))], axis=0).astype(BF16)
    lam = jnp.stack([lr.reshape(ns), li.reshape(ns)], axis=0)
    tb = min(tb, seq)
    full = lambda a: pl.BlockSpec(a.shape, lambda t: (0,) * a.ndim)
    dsk = d_skip.astype(F32).reshape(1, dg)
    gw = glu_w.astype(BF16)
    gb = glu_b.astype(F32).reshape(1, dg)
    return pl.pallas_call(
        functools.partial(_s5_kernel, tb=tb, bsz=bsz, ns=ns, lane_chunk=512),
        grid=(seq // tb,),
        in_specs=[pl.BlockSpec((tb * bsz, dg), lambda t: (t, 0)), full(bbar), full(lam), full(cmat), full(dsk),
                  full(gw), full(gb)],
        out_specs=pl.BlockSpec((tb * bsz, dg), lambda t: (t, 0)),
        out_shape=jax.ShapeDtypeStruct((seq * bsz, dg), BF16),
        scratch_shapes=[pltpu.VMEM((tb * bsz, 2 * ns), F32), pltpu.VMEM((bsz, 2 * ns), F32)],
        compiler_params=_params(("arbitrary",)),
        name="s5",
    )(u_tm, bbar, lam, cmat, dsk, gw, gb)


def _sb_kernel(q_ref, k_ref, v_ref, tri_ref, cmr_ref, o_ref, qb_ref, kb_ref, vb_ref, z0_ref, z1_ref, pre0_ref,
               pre1_ref, tot0_ref, tot1_ref, *, tq, nq, scale):
    qb_ref[...] = (q_ref[...] * scale).astype(BF16)
    kb_ref[...] = k_ref[...].astype(BF16)
    vb_ref[...] = v_ref[...].astype(BF16)
    z_refs, pre_refs, tot_refs = (z0_ref, z1_ref), (pre0_ref, pre1_ref), (tot0_ref, tot1_ref)
    for ref in z_refs + pre_refs + tot_refs:
        ref[...] = jnp.zeros_like(ref)
    tri = tri_ref[...]
    cmr = cmr_ref[...]
    n_steps = nq * (nq + 1) // 2 + 2

    def rows(blk):
        return pl.ds(pl.multiple_of(blk * tq, tq), tq)

    def step(c, slot):
        qa, ia, qb, ib, qc, ic, carry, acc = c
        qa_c = jnp.minimum(qa, nq - 1)
        z_refs[slot][...] = _dot_nt(qb_ref[rows(qa_c), :], kb_ref[rows(jnp.maximum(qa_c - ia, 0)), :])
        below = cmr < ib * tq
        z = jnp.where(below, z_refs[1 - slot][...], -jnp.inf)
        nz = -z
        log_rest = jnp.minimum(nz, 0.0) - jnp.log2(1.0 + jnp.exp2(jnp.minimum(z, nz)))
        hi = log_rest.astype(BF16)
        lo = (log_rest - hi.astype(F32)).astype(BF16)
        after = _dot(hi, tri) + _dot(lo, tri)
        pre_refs[1 - slot][...] = (z + log_rest) + after
        tot_refs[1 - slot][...] = jnp.broadcast_to(jnp.sum(log_rest, axis=-1, keepdims=True), (tq, SB_DIM))
        first = ic == 0
        carry = jnp.where(first, 0.0, carry)
        acc = jnp.where(first, 0.0, acc)
        w = jnp.exp2(pre_refs[slot][...] + jnp.concatenate([carry] * (tq // SB_DIM), axis=1))
        acc = acc + _dot(w.astype(BF16), vb_ref[rows(jnp.maximum(qc - ic, 0)), :])
        o_ref[rows(qc), :] = acc.astype(o_ref.dtype)
        wrap = ia >= qa
        return (jnp.where(wrap, qa + 1, qa), jnp.where(wrap, 0, ia + 1), qa_c, ia, qb, ib,
                carry + tot_refs[slot][...], acc)

    zero = jnp.int32(0)
    c = (zero, zero, zero, zero, zero, zero, jnp.zeros((tq, SB_DIM), F32), jnp.zeros((tq, SB_DIM), F32))
    first_slot = 0
    if n_steps % 2:
        c = step(c, 0)
        first_slot = 1
    lax.fori_loop(0, n_steps // 2, lambda _, c: step(step(c, first_slot), 1 - first_slot), c)


def _stick_breaking(proj, bsz, seq, dg, col0, tq):
    n_heads = dg // SB_DIM
    tq = min(tq, seq)
    nq = seq // tq
    r = jnp.arange(tq, dtype=jnp.int32)
    tri = (r[:, None] > r[None, :]).astype(BF16)
    cmr = r[None, :] - r[:, None]
    cb = col0 // SB_DIM
    seq_blk = lambda j: pl.BlockSpec((seq, SB_DIM), lambda b, h, j=j: (b, cb + j * n_heads + h))
    const = lambda a: pl.BlockSpec(a.shape, lambda b, h: (0, 0))
    return pl.pallas_call(
        functools.partial(_sb_kernel, tq=tq, nq=nq, scale=SB_DIM ** -0.5 * math.log2(math.e)),
        grid=(bsz, n_heads),
        in_specs=[seq_blk(0), seq_blk(1), seq_blk(2), const(tri), const(cmr)],
        out_specs=pl.BlockSpec((seq, SB_DIM), lambda b, h: (b, h)),
        out_shape=jax.ShapeDtypeStruct((bsz * seq, dg), BF16),
        scratch_shapes=[pltpu.VMEM((seq, SB_DIM), BF16), pltpu.VMEM((seq, SB_DIM), BF16),
                        pltpu.VMEM((seq, SB_DIM), BF16)] + [pltpu.VMEM((tq, tq), F32)] * 4
                       + [pltpu.VMEM((tq, SB_DIM), F32)] * 2,
        compiler_params=_params(("parallel", "parallel")),
        name="stick_breaking",
    )(proj, proj, proj, tri, cmr)


def _rwkv7_kernel(r_ref, k_ref, v_ref, tail_ref, mu_r_ref, mu_k_ref, mu_v_ref, mu_t_ref, w0_ref, w2_ref, a0_ref,
                  a2_ref, g2_ref, kk_ref, ka_ref, rk_ref, gng_ref, gnb_ref, tri_ref, o_ref,
                  mt_ref, prev_ref, prev_t_ref, *, n_pairs):
    @pl.when(pl.program_id(1) == 0)
    def _():
        mt_ref[...] = jnp.zeros_like(mt_ref)
        prev_ref[...] = jnp.zeros_like(prev_ref)
        prev_t_ref[...] = jnp.zeros_like(prev_t_ref)

    def shifted(p, prev_row, mu):
        rolled = pltpu.roll(p, 1, axis=0)
        first = lax.broadcasted_iota(jnp.int32, p.shape, 0) == 0
        prev = jnp.where(first, prev_row, rolled)
        return p + mu * (prev - p)

    pr, pk, pv, pt = r_ref[...], k_ref[...], v_ref[...], tail_ref[...]
    tb = pr.shape[0]
    r = shifted(pr, prev_ref[0:1, :], mu_r_ref[...])
    k = shifted(pk, prev_ref[1:2, :], mu_k_ref[...])
    v = shifted(pv, prev_ref[2:3, :], mu_v_ref[...])
    tail = shifted(pt, prev_t_ref[...], mu_t_ref[...])
    prev_ref[0:1, :] = pr[tb - 1:tb, :]
    prev_ref[1:2, :] = pk[tb - 1:tb, :]
    prev_ref[2:3, :] = pv[tb - 1:tb, :]
    prev_t_ref[...] = pt[tb - 1:tb, :]

    xw = tail[:, 0:RW_W_RANK]
    xa = tail[:, RW_W_RANK:RW_W_RANK + RW_A_RANK]
    xg = tail[:, RW_W_RANK + RW_A_RANK:]
    w_log = -_softplus(-(w0_ref[...] + _dot(jnp.tanh(xw).astype(BF16), w2_ref[...]))) - 0.5
    logw = -jnp.exp(w_log)
    a = _sigmoid(a0_ref[...] + _dot(xa.astype(BF16), a2_ref[...]))
    g = _dot(_sigmoid(xg).astype(BF16), g2_ref[...])
    pairs = [slice(p * 2 * RW_DIM, (p + 1) * 2 * RW_DIM) for p in range(n_pairs)]
    chunks = [slice(c * CHUNK, (c + 1) * CHUNK) for c in range(tb // CHUNK)]

    def head_sum(x):
        head0 = lax.broadcasted_iota(jnp.int32, x.shape, 1) < RW_DIM
        s0 = jnp.sum(jnp.where(head0, x, 0.0), axis=-1, keepdims=True)
        s1 = jnp.sum(jnp.where(head0, 0.0, x), axis=-1, keepdims=True)
        return jnp.where(head0, s0, s1)

    def head_sums(x):
        return jnp.concatenate([head_sum(x[:, ps]) for ps in pairs], axis=1)

    kk = k * kk_ref[...]
    kk = kk * lax.rsqrt(jnp.maximum(head_sums(kk * kk), 1e-24))
    k2 = k * (1.0 + (a - 1.0) * ka_ref[...])
    beta = kk * a
    lp = _dot_exact_lhs(tri_ref[...], logw)
    e_pos = jnp.exp(lp)
    e_neg = jnp.exp(-lp)
    e_end = jnp.concatenate([jnp.exp(lp[rc][CHUNK - 1:CHUNK, :] - lp[rc]) for rc in chunks], axis=0)
    r_b = r * e_pos
    a_b = kk * jnp.exp(lp - logw)
    k_b = k2 * e_neg
    b_b = beta * e_neg
    k_t = k2 * e_end
    b_t = beta * e_end
    bonus = head_sums(r * k2 * rk_ref[...]) * v

    n2 = 2 * CHUNK
    ri = lax.broadcasted_iota(jnp.int32, (n2, n2), 0)
    ci = lax.broadcasted_iota(jnp.int32, (n2, n2), 1)
    same = (ri < CHUNK) == (ci < CHUNK)
    strict = jnp.logical_and(same, ci < ri)
    incl = jnp.logical_and(same, ci <= ri)
    eye = (ri == ci).astype(F32)
    head0 = lax.broadcasted_iota(jnp.int32, (CHUNK, 2 * RW_DIM), 1) < RW_DIM

    def stack(x):
        return jnp.concatenate([jnp.where(head0, x, 0.0), jnp.where(head0, 0.0, x)], axis=0)

    def stack2(x, y):
        return jnp.concatenate([stack(x), stack(y)], axis=0).astype(BF16)

    items = [(rc, ps) for rc in chunks for ps in pairs]
    ar = [stack2(a_b[rc, ps], r_b[rc, ps]) for rc, ps in items]
    kb = [stack2(k_b[rc, ps], b_b[rc, ps]) for rc, ps in items]
    v_s = [stack(v[rc, ps]).astype(BF16) for rc, ps in items]
    kbt = [stack2(k_t[rc, ps], b_t[rc, ps]) for rc, ps in items]
    gg = [_dot_nt(x_, y_) for x_, y_ in zip(ar, kb)]
    lg = [jnp.concatenate([jnp.where(strict, g_[:n2, :n2], 0.0), jnp.where(incl, g_[n2:, :n2], 0.0)],
                          axis=0).astype(BF16) for g_ in gg]
    lgv = [_dot(x_, y_) for x_, y_ in zip(lg, v_s)]
    gb = [jnp.where(incl, g_[n2:, n2:], 0.0).astype(BF16) for g_ in gg]
    x = [jnp.where(strict, -g_[:n2, n2:], 0.0) for g_ in gg]
    t_inv = [eye + x_ for x_ in x]
    for _ in range(int(math.log2(CHUNK)) - 1):
        xb = [x_.astype(BF16) for x_ in x]
        x = [_dot(x_, x_) for x_ in xb]
        t_inv = [t_ + _dot(t_.astype(BF16), x_.astype(BF16)) for t_, x_ in zip(t_inv, x)]
    t_inv = [t_.astype(BF16) for t_ in t_inv]

    mt = [mt_ref[p] for p in range(n_pairs)]
    for c, rc in enumerate(chunks):
        sel = range(c * n_pairs, (c + 1) * n_pairs)
        am = [_dot_nt(ar[i], mt[p].astype(BF16)) for p, i in enumerate(sel)]
        u = [_dot(t_inv[i], (am[p][:n2] + lgv[i][:n2]).astype(BF16)) for p, i in enumerate(sel)]
        o_s = [am[p][n2:] + lgv[i][n2:] - _dot(gb[i], u[p].astype(BF16)) for p, i in enumerate(sel)]
        for p, i in enumerate(sel):
            ps = pairs[p]
            vu = jnp.concatenate([v_s[i], (-u[p]).astype(BF16)], axis=0)
            mt[p] = mt[p] * e_pos[rc, ps][CHUNK - 1:CHUNK, :] + _dot_tn(vu, kbt[i])
            out = o_s[p][:CHUNK, :] + o_s[p][CHUNK:, :]
            d = out - head_sum(out) * (1.0 / RW_DIM)
            var = head_sum(d * d) * (1.0 / RW_DIM)
            out = d * lax.rsqrt(var + RW_GN_EPS) * gng_ref[:, ps] + gnb_ref[:, ps]
            o_ref[rc, ps] = ((out + bonus[rc, ps]) * g[rc, ps]).astype(o_ref.dtype)
    for p in range(n_pairs):
        mt_ref[p] = mt[p]


def _rwkv7(proj, mu, w0, w2, a0, a2, g2, k_k, k_a, r_k, gn_g, gn_b, bsz, seq, dg, col0, tb):
    n_heads = dg // RW_DIM
    n_pairs = n_heads // 2
    tb = min(tb, seq)
    nt = seq // tb
    tail_w = RW_W_RANK + RW_A_RANK + RW_G_RANK
    cb = col0 // dg
    tb_col = (col0 + 3 * dg) // tail_w
    muf = mu.astype(F32)
    vec = lambda a: a.astype(F32).reshape(1, -1)
    rr = jnp.arange(tb)
    tri = jnp.logical_and(rr[:, None] >= rr[None, :], rr[:, None] // CHUNK == rr[None, :] // CHUNK).astype(BF16)
    rowblk = lambda j: pl.BlockSpec((tb, dg), lambda b, t, j=j: (b * nt + t, cb + j))
    full = lambda a: pl.BlockSpec(a.shape, lambda b, t: (0,) * a.ndim)
    args = [vec(muf[0:dg]), vec(muf[dg:2 * dg]), vec(muf[2 * dg:3 * dg]), vec(muf[3 * dg:]),
            vec(w0), w2.astype(BF16), vec(a0), a2.astype(BF16), g2.astype(BF16), vec(k_k), vec(k_a), vec(r_k),
            vec(gn_g), vec(gn_b), tri]
    return pl.pallas_call(
        functools.partial(_rwkv7_kernel, n_pairs=n_pairs),
        grid=(bsz, nt),
        in_specs=[rowblk(0), rowblk(1), rowblk(2),
                  pl.BlockSpec((tb, tail_w), lambda b, t: (b * nt + t, tb_col))] + [full(a) for a in args],
        out_specs=pl.BlockSpec((tb, dg), lambda b, t: (b * nt + t, 0)),
        out_shape=jax.ShapeDtypeStruct((bsz * seq, dg), BF16),
        scratch_shapes=[pltpu.VMEM((n_pairs, 2 * RW_DIM, 2 * RW_DIM), F32), pltpu.VMEM((8, dg), F32),
                        pltpu.VMEM((1, tail_w), F32)],
        compiler_params=_params(("parallel", "arbitrary")),
        name="rwkv7",
    )(proj, proj, proj, proj, *args)


def kernel(x, c, ada_w, ada_b, w_in, w_out, hg_lb_logits, hg_norm_g, s5_a_re, s5_a_im, s5_log_dt, s5_b_re, s5_b_im, s5_c_re, s5_c_im, s5_d, s5_glu_w, s5_glu_b, rw_mu, rw_w0, rw_w2, rw_a0, rw_a2, rw_g2, rw_k_k, rw_k_a, rw_r_k, rw_gn_g, rw_gn_b, ln1_g, ln1_b, ffn_w1, ffn_w3, ffn_w2, ln2_g, ln2_b):
    bsz, seq, d = x.shape
    depth = w_in.shape[0]
    n_in = w_in.shape[2]
    dg = d // N_MIXERS
    d_ff = ffn_w1.shape[2]
    alpha = (2 * depth) ** 0.25
    tokens = bsz * seq

    lb_all = jnp.cumsum(jax.nn.softmax(hg_lb_logits.astype(F32), axis=0), axis=0)
    lb_all = lb_all - lb_all[:1]

    mod = _ada_mod(c.astype(F32), ada_w, ada_b)
    mod = mod.reshape(depth, bsz, 6, 1, d)
    shift1, scale1, gate1, shift2, scale2, gate2 = [mod[:, :, i] for i in range(6)]
    zeros_vec = jnp.zeros((bsz, 1, d), F32)

    n_pad = -(-n_in // 1024) * 1024
    x2 = x.reshape(tokens, d)
    h = _modulate(x2, scale1[0], shift1[0], seq)
    for l in range(depth):
        w_in_l = jnp.pad(w_in[l].astype(BF16), ((0, 0), (0, n_pad - n_in)))
        proj, u_tm = _in_proj(h, w_in_l, seq, 4 * dg, dg, 1024, 1024)
        o_a = _hgrn2(proj, lb_all[l], hg_norm_g[l], bsz, seq, dg, 256)
        o_b = _s5(u_tm.reshape(tokens, dg), s5_a_re[l], s5_a_im[l], s5_log_dt[l], s5_b_re[l], s5_b_im[l], s5_c_re[l],
                  s5_c_im[l], s5_d[l], s5_glu_w[l], s5_glu_b[l], bsz, 64)
        o_b = o_b.reshape(seq, bsz * dg)
        o_c = _stick_breaking(proj, bsz, seq, dg, 5 * dg, 256)
        o_d = _rwkv7(proj, rw_mu[l], rw_w0[l], rw_w2[l], rw_a0[l], rw_a2[l], rw_g2[l], rw_k_k[l], rw_k_a[l],
                     rw_r_k[l].reshape(-1), rw_gn_g[l], rw_gn_b[l], bsz, seq, dg, 8 * dg, 256)
        x2, h = _out_proj_norm([o_a, o_b, o_c, o_d], w_out[l].astype(BF16), x2, gate1[l],
                               ln1_g[l].astype(F32).reshape(1, d), ln1_b[l].astype(F32).reshape(1, d),
                               scale2[l], shift2[l], seq, alpha, 256)
        u = _ffn_up(h, ffn_w1[l].astype(BF16), ffn_w3[l].astype(BF16), 1024, 512)
        last = l == depth - 1
        x2, h = _ffn_down_norm(u, ffn_w2[l].astype(BF16), x2, gate2[l],
                               ln2_g[l].astype(F32).reshape(1, d), ln2_b[l].astype(F32).reshape(1, d),
                               zeros_vec if last else scale1[l + 1], zeros_vec if last else shift1[l + 1],
                               seq, alpha, 256)
    return x2.reshape(bsz, seq, d)
```

```python
import functools
import math

import jax
import jax.numpy as jnp
from jax import lax
from jax.experimental import pallas as pl
from jax.experimental.pallas import tpu as pltpu

F32 = jnp.float32
BF16 = jnp.bfloat16

N_MIXERS = 4
CHUNK = 64
SUB = 16
HG_DIM = 128
S5_BLOCK = 256
SB_DIM = 128
RW_DIM = 64
RW_W_RANK = 64
RW_A_RANK = 64
RW_G_RANK = 128
RW_GN_EPS = 64e-5
LN_EPS = 1e-5
RMS_EPS = 1e-6
LOG2E = math.log2(math.e)
V7X_VMEM_LIMIT = 56 * 1024 * 1024


def _params(sem, vmem=V7X_VMEM_LIMIT):
    return pltpu.CompilerParams(dimension_semantics=sem, vmem_limit_bytes=vmem)


def _dot(a, b):
    return jnp.dot(a, b, preferred_element_type=F32)


def _dot_nt(a, b):
    return lax.dot_general(a, b, (((1,), (1,)), ((), ())), preferred_element_type=F32)


def _dot_tn(a, b):
    return lax.dot_general(a, b, (((0,), (0,)), ((), ())), preferred_element_type=F32)


def _split3(x):
    h = x.astype(BF16)
    r = x - h.astype(F32)
    m = r.astype(BF16)
    lo = (r - m.astype(F32)).astype(BF16)
    return h, m, lo


def _dot_exact_lhs(a_bf16, x):
    h, m, lo = _split3(x)
    return _dot(a_bf16, h) + _dot(a_bf16, m) + _dot(a_bf16, lo)


def _dot_exact_rhs(x, b_bf16):
    h, m, lo = _split3(x)
    return _dot(h, b_bf16) + _dot(m, b_bf16) + _dot(lo, b_bf16)


def _softplus(x):
    return jnp.maximum(x, 0.0) + jnp.log1p(jnp.exp(-jnp.abs(x)))


def _sigmoid(x):
    return jax.nn.sigmoid(x)


def _layer_norm_rows(y, g, b):
    mu = jnp.mean(y, axis=-1, keepdims=True)
    d = y - mu
    var = jnp.mean(d * d, axis=-1, keepdims=True)
    return d * lax.rsqrt(var + LN_EPS) * g + b


def _ada_kernel(c_ref, w_ref, b_ref, o_ref):
    c = c_ref[...]
    ca = (c * _sigmoid(c)).astype(BF16)
    o_ref[0] = _dot(ca, w_ref[0].astype(BF16)) + b_ref[0]


def _ada_mod(c, ada_w, ada_b):
    depth, d, n = ada_w.shape
    bsz = c.shape[0]
    tn = 1024
    return pl.pallas_call(
        _ada_kernel,
        grid=(depth, n // tn),
        in_specs=[pl.BlockSpec((bsz, d), lambda l, j: (0, 0)),
                  pl.BlockSpec((1, d, tn), lambda l, j: (l, 0, j)),
                  pl.BlockSpec((1, 1, tn), lambda l, j: (l, 0, j))],
        out_specs=pl.BlockSpec((1, bsz, tn), lambda l, j: (l, 0, j)),
        out_shape=jax.ShapeDtypeStruct((depth, bsz, n), F32),
        compiler_params=_params(("parallel", "parallel")),
        name="ada_mod",
    )(c, ada_w, ada_b.reshape(depth, 1, n))


def _modulate_kernel(x_ref, sc_ref, sh_ref, o_ref):
    o_ref[...] = (x_ref[...] * (1.0 + sc_ref[0]) + sh_ref[0]).astype(o_ref.dtype)


def _modulate(x2, scale, shift, seq):
    t, d = x2.shape
    tm = min(1024, seq)
    per = seq // tm
    vec = pl.BlockSpec((1, 1, d), lambda i: (i // per, 0, 0))
    return pl.pallas_call(
        _modulate_kernel,
        grid=(t // tm,),
        in_specs=[pl.BlockSpec((tm, d), lambda i: (i, 0)), vec, vec],
        out_specs=pl.BlockSpec((tm, d), lambda i: (i, 0)),
        out_shape=jax.ShapeDtypeStruct((t, d), BF16),
        compiler_params=_params(("parallel",)),
        name="modulate",
    )(x2, scale, shift)


def _in_proj_kernel(a_ref, w_ref, o_ref, u_ref, *, j_u, off_u):
    acc = _dot(a_ref[...], w_ref[...])
    o_ref[...] = acc

    @pl.when(pl.program_id(1) == j_u)
    def _():
        u_ref[...] = acc[:, off_u:off_u + u_ref.shape[1]]


def _in_proj(a, w, seq, col_u, dg, tm, tn):
    m, k = a.shape
    n = w.shape[1]
    tm = min(tm, seq)
    per = seq // tm
    j_u, off_u = col_u // tn, col_u % tn
    assert off_u + dg <= tn
    return pl.pallas_call(
        functools.partial(_in_proj_kernel, j_u=j_u, off_u=off_u),
        grid=(m // tm, n // tn),
        in_specs=[pl.BlockSpec((tm, k), lambda i, j: (i, 0)),
                  pl.BlockSpec((k, tn), lambda i, j: (0, j))],
        out_specs=[pl.BlockSpec((tm, tn), lambda i, j: (i, j)),
                   pl.BlockSpec((tm, dg), lambda i, j: (i % per, i // per))],
        out_shape=[jax.ShapeDtypeStruct((m, n), F32), jax.ShapeDtypeStruct((seq, (m // seq) * dg), F32)],
        compiler_params=_params(("parallel", "arbitrary")),
        name="in_proj",
    )(a, w)


def _ffn_up_kernel(a_ref, w1_ref, w3_ref, o_ref):
    a = a_ref[...]
    p1 = _dot(a, w1_ref[...])
    p3 = _dot(a, w3_ref[...])
    o_ref[...] = (p1 * _sigmoid(p1) * p3).astype(o_ref.dtype)


def _ffn_up(a, w1, w3, tm, tn):
    m, k = a.shape
    n = w1.shape[1]
    tm = min(tm, m)
    wspec = pl.BlockSpec((k, tn), lambda i, j: (0, j))
    return pl.pallas_call(
        _ffn_up_kernel,
        grid=(m // tm, n // tn),
        in_specs=[pl.BlockSpec((tm, k), lambda i, j: (i, 0)), wspec, wspec],
        out_specs=pl.BlockSpec((tm, tn), lambda i, j: (i, j)),
        out_shape=jax.ShapeDtypeStruct((m, n), BF16),
        compiler_params=_params(("parallel", "parallel")),
        name="ffn_up",
    )(a, w1, w3)


def _residual_norm(acc, x_ref, gate_ref, lng_ref, lnb_ref, sc_ref, sh_ref, xo_ref, ho_ref, alpha):
    y = alpha * x_ref[...] + gate_ref[0] * acc
    xn = _layer_norm_rows(y, lng_ref[...], lnb_ref[...])
    xo_ref[...] = xn
    ho_ref[...] = (xn * (1.0 + sc_ref[0]) + sh_ref[0]).astype(ho_ref.dtype)


def _out_proj_kernel(oa_ref, ob_ref, oc_ref, od_ref, w_ref, x_ref, gate_ref, lng_ref, lnb_ref, sc_ref, sh_ref,
                     xo_ref, ho_ref, *, alpha, dg):
    acc = _dot(oa_ref[...], w_ref[0 * dg:1 * dg, :])
    acc += _dot(ob_ref[...], w_ref[1 * dg:2 * dg, :])
    acc += _dot(oc_ref[...], w_ref[2 * dg:3 * dg, :])
    acc += _dot(od_ref[...], w_ref[3 * dg:4 * dg, :])
    _residual_norm(acc, x_ref, gate_ref, lng_ref, lnb_ref, sc_ref, sh_ref, xo_ref, ho_ref, alpha)


def _out_proj_norm(outs, w, x2, gate, lng, lnb, scale, shift, seq, alpha, tm):
    t, d = x2.shape
    dg = outs[0].shape[1]
    tm = min(tm, seq)
    per = seq // tm
    row = lambda i: (i, 0)
    vec = pl.BlockSpec((1, 1, d), lambda i: (i // per, 0, 0))
    par = pl.BlockSpec((1, d), lambda i: (0, 0))
    return pl.pallas_call(
        functools.partial(_out_proj_kernel, alpha=alpha, dg=dg),
        grid=(t // tm,),
        in_specs=[pl.BlockSpec((tm, dg), row), pl.BlockSpec((tm, dg), lambda i: (i % per, i // per)),
                  pl.BlockSpec((tm, dg), row), pl.BlockSpec((tm, dg), row)] + [
            pl.BlockSpec(w.shape, lambda i: (0, 0), pipeline_mode=pl.Buffered(1)),
            pl.BlockSpec((tm, d), row), vec, par, par, vec, vec],
        out_specs=[pl.BlockSpec((tm, d), row), pl.BlockSpec((tm, d), row)],
        out_shape=[jax.ShapeDtypeStruct((t, d), F32), jax.ShapeDtypeStruct((t, d), BF16)],
        compiler_params=_params(("parallel",)),
        name="out_proj_norm",
    )(*outs, w, x2, gate, lng, lnb, scale, shift)


def _ffn_down_kernel(u_ref, w_ref, x_ref, gate_ref, lng_ref, lnb_ref, sc_ref, sh_ref, xo_ref, ho_ref, *, alpha):
    acc = _dot(u_ref[...], w_ref[...])
    _residual_norm(acc, x_ref, gate_ref, lng_ref, lnb_ref, sc_ref, sh_ref, xo_ref, ho_ref, alpha)


def _ffn_down_norm(u, w, x2, gate, lng, lnb, scale, shift, seq, alpha, tm):
    t, d = x2.shape
    kk = u.shape[1]
    tm = min(tm, seq)
    per = seq // tm
    row = lambda i: (i, 0)
    vec = pl.BlockSpec((1, 1, d), lambda i: (i // per, 0, 0))
    par = pl.BlockSpec((1, d), lambda i: (0, 0))
    return pl.pallas_call(
        functools.partial(_ffn_down_kernel, alpha=alpha),
        grid=(t // tm,),
        in_specs=[pl.BlockSpec((tm, kk), row),
                  pl.BlockSpec((kk, d), lambda i: (0, 0), pipeline_mode=pl.Buffered(1)),
                  pl.BlockSpec((tm, d), row), vec, par, par, vec, vec],
        out_specs=[pl.BlockSpec((tm, d), row), pl.BlockSpec((tm, d), row)],
        out_shape=[jax.ShapeDtypeStruct((t, d), F32), jax.ShapeDtypeStruct((t, d), BF16)],
        compiler_params=_params(("parallel",)),
        name="ffn_down_norm",
    )(u, w, x2, gate, lng, lnb, scale, shift)


def _hgrn2_kernel(q_ref, f_ref, i_ref, g_ref, la_ref, lc_ref, oml_ref, ng_ref, tri_ref, o_ref, st_ref, *, n_heads, n_chunks):
    @pl.when(pl.program_id(1) == 0)
    def _():
        st_ref[...] = jnp.zeros_like(st_ref)

    la = la_ref[...]
    lc = lc_ref[...]
    oml = oml_ref[...]
    ng = ng_ref[...]
    tri = tri_ref[...]
    n_sub = CHUNK // SUB
    tio = lax.broadcasted_iota(jnp.int32, (n_sub, SUB, HG_DIM), 1)
    rio = lax.broadcasted_iota(jnp.int32, (CHUNK, n_heads * HG_DIM), 0)

    def chunk_body(c, carry):
        rows = pl.ds(pl.multiple_of(c * CHUNK, CHUNK), CHUNK)
        z = f_ref[rows, :]
        log_sig = jnp.minimum(z, 0.0) - jnp.log1p(jnp.exp(-jnp.abs(z)))
        y = lc + log_sig
        mx = jnp.maximum(la, y)
        log_f = mx + jnp.log1p(jnp.exp(-jnp.abs(la - y)))
        kg = oml * _sigmoid(-z)
        q = q_ref[rows, :]
        qf = q * _sigmoid(q)
        v = i_ref[rows, :]
        g = g_ref[rows, :]
        b = _dot_exact_lhs(tri, log_f) * LOG2E
        b_last = b[CHUNK - 1:CHUNK, :]
        e_last = jnp.exp2(b_last)
        k_end = (kg * jnp.exp2(b_last - b)).astype(BF16)
        q_in = (qf * jnp.exp2(b)).astype(BF16)
        vb = v.astype(BF16)
        heads = [slice(h * HG_DIM, (h + 1) * HG_DIM) for h in range(n_heads)]
        st = [st_ref[h] for h in range(n_heads)]
        o = [_dot_nt(q_in[:, hs], st_.astype(BF16)) for hs, st_ in zip(heads, st)]
        split = lambda x: [x[:, hs].reshape(n_sub, SUB, HG_DIM) for hs in heads]
        b3, q3, k3, v3 = split(b), split(qf), split(kg), split(v)
        acc = [jnp.zeros((n_sub, SUB, HG_DIM), F32) for _ in heads]
        for s in range(SUB):
            causal = tio >= s
            for h in range(n_heads):
                e = jnp.exp2(jnp.where(causal, b3[h] - b3[h][:, s:s + 1, :], -jnp.inf))
                w = jnp.sum(q3[h] * e * k3[h][:, s:s + 1, :], axis=-1, keepdims=True)
                acc[h] = acc[h] + w * v3[h][:, s:s + 1, :]
        parts = [[jnp.zeros((SUB, HG_DIM), F32)] for _ in heads]
        for i in range(1, n_sub):
            rho = b[i * SUB - 1:i * SUB, :]
            qi = (qf[i * SUB:(i + 1) * SUB, :] * jnp.exp2(b[i * SUB:(i + 1) * SUB, :] - rho)).astype(BF16)
            ki = (kg * jnp.exp2(jnp.where(rio < i * SUB, rho - b, -jnp.inf))).astype(BF16)
            a = [_dot_nt(qi[:, hs], ki[:, hs]).astype(BF16) for hs in heads]
            for h, hs in enumerate(heads):
                parts[h].append(_dot(a[h], vb[:, hs]))
        gs = g * _sigmoid(g)
        for h, hs in enumerate(heads):
            st_ref[h] = st[h] * e_last[:, hs] + _dot_tn(vb[:, hs], k_end[:, hs])
            oh = o[h] + acc[h].reshape(CHUNK, HG_DIM) + jnp.concatenate(parts[h], axis=0)
            oh = oh * lax.rsqrt(jnp.mean(oh * oh, axis=-1, keepdims=True) + RMS_EPS) * ng[:, hs]
            o_ref[rows, hs] = (oh * gs[:, hs]).astype(o_ref.dtype)
        return carry

    lax.fori_loop(0, n_chunks, chunk_body, 0)


def _hgrn2(proj, lb, norm_g, bsz, seq, dg, tb):
    n_heads = dg // HG_DIM
    tb = min(tb, seq)
    nt = seq // tb
    lbf = lb.astype(F32).reshape(1, dg)
    la = jnp.log(jnp.maximum(lbf, 0.0))
    lc = jnp.log1p(-lbf)
    oml = 1.0 - lbf
    r = jnp.arange(CHUNK)
    tri = (r[:, None] >= r[None, :]).astype(BF16)
    col = lambda j: pl.BlockSpec((tb, dg), lambda b, t, j=j: (b * nt + t, j))
    par = pl.BlockSpec((1, dg), lambda b, t: (0, 0))
    return pl.pallas_call(
        functools.partial(_hgrn2_kernel, n_heads=n_heads, n_chunks=tb // CHUNK),
        grid=(bsz, nt),
        in_specs=[col(0), col(1), col(2), col(3), par, par, par, par,
                  pl.BlockSpec((CHUNK, CHUNK), lambda b, t: (0, 0))],
        out_specs=pl.BlockSpec((tb, dg), lambda b, t: (b * nt + t, 0)),
        out_shape=jax.ShapeDtypeStruct((bsz * seq, dg), BF16),
        scratch_shapes=[pltpu.VMEM((n_heads, HG_DIM, HG_DIM), F32)],
        compiler_params=_params(("parallel", "arbitrary")),
        name="hgrn2",
    )(proj, proj, proj, proj, la, lc, oml, norm_g.astype(F32).reshape(1, dg), tri)


def _s5_kernel(u_ref, bbar_ref, lam_ref, cmat_ref, dskip_ref, gw_ref, gb_ref, o_ref, xs_ref, h_ref, *, tb, bsz, ns, lane_chunk):
    @pl.when(pl.program_id(0) == 0)
    def _():
        h_ref[...] = jnp.zeros_like(h_ref)

    u = u_ref[...]
    ub = u.astype(BF16)
    dg = u.shape[1]
    n_blk = dg // S5_BLOCK
    sb = ns // n_blk
    blocks = [(slice(b * S5_BLOCK, (b + 1) * S5_BLOCK), slice(b * sb, (b + 1) * sb),
               slice(ns + b * sb, ns + (b + 1) * sb)) for b in range(n_blk)]
    for cs, re, im in blocks:
        xs_ref[:, re] = _dot(ub[:, cs], bbar_ref[cs, re])
        xs_ref[:, im] = _dot(ub[:, cs], bbar_ref[cs, im])
    for c in range(ns // lane_chunk):
        re = slice(c * lane_chunk, (c + 1) * lane_chunk)
        im = slice(ns + c * lane_chunk, ns + (c + 1) * lane_chunk)
        lr = jnp.broadcast_to(lam_ref[0:1, re], (bsz, lane_chunk))
        li = jnp.broadcast_to(lam_ref[1:2, re], (bsz, lane_chunk))

        def step(t, carry):
            hr, hi = carry
            rows = pl.ds(pl.multiple_of(t * bsz, bsz), bsz)
            nr = lr * hr - li * hi + xs_ref[rows, re]
            ni = lr * hi + li * hr + xs_ref[rows, im]
            xs_ref[rows, re] = nr
            xs_ref[rows, im] = ni
            return nr, ni

        hr, hi = lax.fori_loop(0, tb, step, (h_ref[:, re], h_ref[:, im]), unroll=4)
        h_ref[:, re] = hr
        h_ref[:, im] = hi
    y = jnp.concatenate([_dot(xs_ref[:, re].astype(BF16), cmat_ref[re, cs])
                         + _dot(xs_ref[:, im].astype(BF16), cmat_ref[im, cs]) for cs, re, im in blocks], axis=1)
    y = y + dskip_ref[...] * u
    y = 0.5 * y * (1.0 + jnp.tanh(math.sqrt(2.0 / math.pi) * (y + 0.044715 * (y * y * y))))
    gate = _sigmoid(_dot(y.astype(BF16), gw_ref[...]) + gb_ref[...])
    o_ref[...] = (y * gate).astype(o_ref.dtype)


def _s5(u_tm, a_re, a_im, log_dt, b_re, b_im, c_re, c_im, d_skip, glu_w, glu_b, bsz, tb):
    dg = u_tm.shape[1]
    seq = u_tm.shape[0] // bsz
    groups, p = a_re.shape
    ch = dg // groups
    ns = groups * p
    dt = jnp.exp(log_dt.astype(F32))[:, None]
    ar, ai = a_re.astype(F32), a_im.astype(F32)
    mag = jnp.exp(ar * dt)
    lr, li = mag * jnp.cos(ai * dt), mag * jnp.sin(ai * dt)
    den = ar * ar + ai * ai
    zr = ((lr - 1.0) * ar + li * ai) / den
    zi = (li * ar - (lr - 1.0) * ai) / den
    br, bi = b_re.astype(F32), b_im.astype(F32)
    bbar_r = zr[..., None] * br - zi[..., None] * bi
    bbar_i = zr[..., None] * bi + zi[..., None] * br
    eye = jnp.eye(groups, dtype=F32)

    def blockdiag_in(w):
        return jnp.einsum('gpc,gh->gchp', w, eye).reshape(groups * ch, groups * p)

    def blockdiag_out(w):
        return jnp.einsum('gcp,gh->gphc', w, eye).reshape(groups * p, groups * ch)

    bbar = jnp.concatenate([blockdiag_in(bbar_r), blockdiag_in(bbar_i)], axis=1).astype(BF16)
    cmat = jnp.concatenate([blockdiag_out(c_re.astype(F32)), -blockdiag_out(c_im.astype(F32))], axis=0).astype(BF16)
    lam = jnp.stack([lr.reshape(ns), li.reshape(ns)], axis=0)
    tb = min(tb, seq)
    full = lambda a: pl.BlockSpec(a.shape, lambda t: (0,) * a.ndim)
    dsk = d_skip.astype(F32).reshape(1, dg)
    gw = glu_w.astype(BF16)
    gb = glu_b.astype(F32).reshape(1, dg)
    return pl.pallas_call(
        functools.partial(_s5_kernel, tb=tb, bsz=bsz, ns=ns, lane_chunk=512),
        grid=(seq // tb,),
        in_specs=[pl.BlockSpec((tb * bsz, dg), lambda t: (t, 0)), full(bbar), full(lam), full(cmat), full(dsk),
                  full(gw), full(gb)],
        out_specs=pl.BlockSpec((tb * bsz, dg), lambda t: (t, 0)),
        out_shape=jax.ShapeDtypeStruct((seq * bsz, dg), BF16),
        scratch_shapes=[pltpu.VMEM((tb * bsz, 2 * ns), F32), pltpu.VMEM((bsz, 2 * ns), F32)],
        compiler_params=_params(("arbitrary",)),
        name="s5",
    )(u_tm, bbar, lam, cmat, dsk, gw, gb)


def _sb_kernel(q_ref, k_ref, v_ref, tri_ref, cmr_ref, o_ref, qb_ref, kb_ref, vb_ref, z0_ref, z1_ref, pre0_ref,
               pre1_ref, tot0_ref, tot1_ref, *, tq, nq, scale):
    qb_ref[...] = (q_ref[...] * scale).astype(BF16)
    kb_ref[...] = k_ref[...].astype(BF16)
    vb_ref[...] = v_ref[...].astype(BF16)
    z_refs, pre_refs, tot_refs = (z0_ref, z1_ref), (pre0_ref, pre1_ref), (tot0_ref, tot1_ref)
    for ref in z_refs + pre_refs + tot_refs:
        ref[...] = jnp.zeros_like(ref)
    tri = tri_ref[...]
    cmr = cmr_ref[...]
    n_steps = nq * (nq + 1) // 2 + 2

    def rows(blk):
        return pl.ds(pl.multiple_of(blk * tq, tq), tq)

    def step(c, slot):
        qa, ia, qb, ib, qc, ic, carry, acc = c
        qa_c = jnp.minimum(qa, nq - 1)
        z_refs[slot][...] = _dot_nt(qb_ref[rows(qa_c), :], kb_ref[rows(jnp.maximum(qa_c - ia, 0)), :])
        below = cmr < ib * tq
        z = jnp.where(below, z_refs[1 - slot][...], -jnp.inf)
        nz = -z
        log_rest = jnp.minimum(nz, 0.0) - jnp.log2(1.0 + jnp.exp2(jnp.minimum(z, nz)))
        after = _dot(log_rest.astype(BF16), tri)
        pre_refs[1 - slot][...] = (z + log_rest) + after
        tot_refs[1 - slot][...] = jnp.broadcast_to(jnp.sum(log_rest, axis=-1, keepdims=True), (tq, SB_DIM))
        first = ic == 0
        carry = jnp.where(first, 0.0, carry)
        acc = jnp.where(first, 0.0, acc)
        w = jnp.exp2(pre_refs[slot][...] + jnp.concatenate([carry] * (tq // SB_DIM), axis=1))
        acc = acc + _dot(w.astype(BF16), vb_ref[rows(jnp.maximum(qc - ic, 0)), :])
        o_ref[rows(qc), :] = acc.astype(o_ref.dtype)
        wrap = ia >= qa
        return (jnp.where(wrap, qa + 1, qa), jnp.where(wrap, 0, ia + 1), qa_c, ia, qb, ib,
                carry + tot_refs[slot][...], acc)

    zero = jnp.int32(0)
    c = (zero, zero, zero, zero, zero, zero, jnp.zeros((tq, SB_DIM), F32), jnp.zeros((tq, SB_DIM), F32))
    first_slot = 0
    if n_steps % 2:
        c = step(c, 0)
        first_slot = 1
    lax.fori_loop(0, n_steps // 2, lambda _, c: step(step(c, first_slot), 1 - first_slot), c)


def _stick_breaking(proj, bsz, seq, dg, col0, tq):
    n_heads = dg // SB_DIM
    tq = min(tq, seq)
    nq = seq // tq
    r = jnp.arange(tq, dtype=jnp.int32)
    tri = (r[:, None] > r[None, :]).astype(BF16)
    cmr = r[None, :] - r[:, None]
    cb = col0 // SB_DIM
    seq_blk = lambda j: pl.BlockSpec((seq, SB_DIM), lambda b, h, j=j: (b, cb + j * n_heads + h))
    const = lambda a: pl.BlockSpec(a.shape, lambda b, h: (0, 0))
    return pl.pallas_call(
        functools.partial(_sb_kernel, tq=tq, nq=nq, scale=SB_DIM ** -0.5 * math.log2(math.e)),
        grid=(bsz, n_heads),
        in_specs=[seq_blk(0), seq_blk(1), seq_blk(2), const(tri), const(cmr)],
        out_specs=pl.BlockSpec((seq, SB_DIM), lambda b, h: (b, h)),
        out_shape=jax.ShapeDtypeStruct((bsz * seq, dg), BF16),
        scratch_shapes=[pltpu.VMEM((seq, SB_DIM), BF16), pltpu.VMEM((seq, SB_DIM), BF16),
                        pltpu.VMEM((seq, SB_DIM), BF16)] + [pltpu.VMEM((tq, tq), F32)] * 4
                       + [pltpu.VMEM((tq, SB_DIM), F32)] * 2,
        compiler_params=_params(("parallel", "parallel")),
        name="stick_breaking",
    )(proj, proj, proj, tri, cmr)


def _rwkv7_kernel(r_ref, k_ref, v_ref, tail_ref, mu_r_ref, mu_k_ref, mu_v_ref, mu_t_ref, w0_ref, w2_ref, a0_ref,
                  a2_ref, g2_ref, kk_ref, ka_ref, rk_ref, gng_ref, gnb_ref, tri_ref, o_ref,
                  mt_ref, prev_ref, prev_t_ref, *, n_pairs):
    @pl.when(pl.program_id(1) == 0)
    def _():
        mt_ref[...] = jnp.zeros_like(mt_ref)
        prev_ref[...] = jnp.zeros_like(prev_ref)
        prev_t_ref[...] = jnp.zeros_like(prev_t_ref)

    def shifted(p, prev_row, mu):
        rolled = pltpu.roll(p, 1, axis=0)
        first = lax.broadcasted_iota(jnp.int32, p.shape, 0) == 0
        prev = jnp.where(first, prev_row, rolled)
        return p + mu * (prev - p)

    pr, pk, pv, pt = r_ref[...], k_ref[...], v_ref[...], tail_ref[...]
    tb = pr.shape[0]
    r = shifted(pr, prev_ref[0:1, :], mu_r_ref[...])
    k = shifted(pk, prev_ref[1:2, :], mu_k_ref[...])
    v = shifted(pv, prev_ref[2:3, :], mu_v_ref[...])
    tail = shifted(pt, prev_t_ref[...], mu_t_ref[...])
    prev_ref[0:1, :] = pr[tb - 1:tb, :]
    prev_ref[1:2, :] = pk[tb - 1:tb, :]
    prev_ref[2:3, :] = pv[tb - 1:tb, :]
    prev_t_ref[...] = pt[tb - 1:tb, :]

    xw = tail[:, 0:RW_W_RANK]
    xa = tail[:, RW_W_RANK:RW_W_RANK + RW_A_RANK]
    xg = tail[:, RW_W_RANK + RW_A_RANK:]
    w_log = -_softplus(-(w0_ref[...] + _dot(jnp.tanh(xw).astype(BF16), w2_ref[...]))) - 0.5
    logw = -jnp.exp(w_log)
    a = _sigmoid(a0_ref[...] + _dot(xa.astype(BF16), a2_ref[...]))
    g = _dot(_sigmoid(xg).astype(BF16), g2_ref[...])
    pairs = [slice(p * 2 * RW_DIM, (p + 1) * 2 * RW_DIM) for p in range(n_pairs)]
    chunks = [slice(c * CHUNK, (c + 1) * CHUNK) for c in range(tb // CHUNK)]

    def head_sum(x):
        head0 = lax.broadcasted_iota(jnp.int32, x.shape, 1) < RW_DIM
        s0 = jnp.sum(jnp.where(head0, x, 0.0), axis=-1, keepdims=True)
        s1 = jnp.sum(jnp.where(head0, 0.0, x), axis=-1, keepdims=True)
        return jnp.where(head0, s0, s1)

    def head_sums(x):
        return jnp.concatenate([head_sum(x[:, ps]) for ps in pairs], axis=1)

    kk = k * kk_ref[...]
    kk = kk * lax.rsqrt(jnp.maximum(head_sums(kk * kk), 1e-24))
    k2 = k * (1.0 + (a - 1.0) * ka_ref[...])
    beta = kk * a
    lp = _dot_exact_lhs(tri_ref[...], logw)
    e_pos = jnp.exp(lp)
    e_neg = jnp.exp(-lp)
    e_end = jnp.concatenate([jnp.exp(lp[rc][CHUNK - 1:CHUNK, :] - lp[rc]) for rc in chunks], axis=0)
    r_b = r * e_pos
    a_b = kk * jnp.exp(lp - logw)
    k_b = k2 * e_neg
    b_b = beta * e_neg
    k_t = k2 * e_end
    b_t = beta * e_end
    bonus = head_sums(r * k2 * rk_ref[...]) * v

    n2 = 2 * CHUNK
    ri = lax.broadcasted_iota(jnp.int32, (n2, n2), 0)
    ci = lax.broadcasted_iota(jnp.int32, (n2, n2), 1)
    same = (ri < CHUNK) == (ci < CHUNK)
    strict = jnp.logical_and(same, ci < ri)
    incl = jnp.logical_and(same, ci <= ri)
    eye = (ri == ci).astype(F32)
    head0 = lax.broadcasted_iota(jnp.int32, (CHUNK, 2 * RW_DIM), 1) < RW_DIM

    def stack(x):
        return jnp.concatenate([jnp.where(head0, x, 0.0), jnp.where(head0, 0.0, x)], axis=0)

    def stack2(x, y):
        return jnp.concatenate([stack(x), stack(y)], axis=0).astype(BF16)

    items = [(rc, ps) for rc in chunks for ps in pairs]
    ar = [stack2(a_b[rc, ps], r_b[rc, ps]) for rc, ps in items]
    kb = [stack2(k_b[rc, ps], b_b[rc, ps]) for rc, ps in items]
    v_s = [stack(v[rc, ps]).astype(BF16) for rc, ps in items]
    kbt = [stack2(k_t[rc, ps], b_t[rc, ps]) for rc, ps in items]
    gg = [_dot_nt(x_, y_) for x_, y_ in zip(ar, kb)]
    lg = [jnp.concatenate([jnp.where(strict, g_[:n2, :n2], 0.0), jnp.where(incl, g_[n2:, :n2], 0.0)],
                          axis=0).astype(BF16) for g_ in gg]
    lgv = [_dot(x_, y_) for x_, y_ in zip(lg, v_s)]
    gb = [jnp.where(incl, g_[n2:, n2:], 0.0).astype(BF16) for g_ in gg]
    x = [jnp.where(strict, -g_[:n2, n2:], 0.0) for g_ in gg]
    t_inv = [eye + x_ for x_ in x]
    for _ in range(int(math.log2(CHUNK)) - 1):
        xb = [x_.astype(BF16) for x_ in x]
        x = [_dot(x_, x_) for x_ in xb]
        t_inv = [t_ + _dot(t_.astype(BF16), x_.astype(BF16)) for t_, x_ in zip(t_inv, x)]
    t_inv = [t_.astype(BF16) for t_ in t_inv]

    mt = [mt_ref[p] for p in range(n_pairs)]
    for c, rc in enumerate(chunks):
        sel = range(c * n_pairs, (c + 1) * n_pairs)
        am = [_dot_nt(ar[i], mt[p].astype(BF16)) for p, i in enumerate(sel)]
        u = [_dot(t_inv[i], (am[p][:n2] + lgv[i][:n2]).astype(BF16)) for p, i in enumerate(sel)]
        o_s = [am[p][n2:] + lgv[i][n2:] - _dot(gb[i], u[p].astype(BF16)) for p, i in enumerate(sel)]
        for p, i in enumerate(sel):
            ps = pairs[p]
            vu = jnp.concatenate([v_s[i], (-u[p]).astype(BF16)], axis=0)
            mt[p] = mt[p] * e_pos[rc, ps][CHUNK - 1:CHUNK, :] + _dot_tn(vu, kbt[i])
            out = o_s[p][:CHUNK, :] + o_s[p][CHUNK:, :]
            d = out - head_sum(out) * (1.0 / RW_DIM)
            var = head_sum(d * d) * (1.0 / RW_DIM)
            out = d * lax.rsqrt(var + RW_GN_EPS) * gng_ref[:, ps] + gnb_ref[:, ps]
            o_ref[rc, ps] = ((out + bonus[rc, ps]) * g[rc, ps]).astype(o_ref.dtype)
    for p in range(n_pairs):
        mt_ref[p] = mt[p]


def _rwkv7(proj, mu, w0, w2, a0, a2, g2, k_k, k_a, r_k, gn_g, gn_b, bsz, seq, dg, col0, tb):
    n_heads = dg // RW_DIM
    n_pairs = n_heads // 2
    tb = min(tb, seq)
    nt = seq // tb
    tail_w = RW_W_RANK + RW_A_RANK + RW_G_RANK
    cb = col0 // dg
    tb_col = (col0 + 3 * dg) // tail_w
    muf = mu.astype(F32)
    vec = lambda a: a.astype(F32).reshape(1, -1)
    rr = jnp.arange(tb)
    tri = jnp.logical_and(rr[:, None] >= rr[None, :], rr[:, None] // CHUNK == rr[None, :] // CHUNK).astype(BF16)
    rowblk = lambda j: pl.BlockSpec((tb, dg), lambda b, t, j=j: (b * nt + t, cb + j))
    full = lambda a: pl.BlockSpec(a.shape, lambda b, t: (0,) * a.ndim)
    args = [vec(muf[0:dg]), vec(muf[dg:2 * dg]), vec(muf[2 * dg:3 * dg]), vec(muf[3 * dg:]),
            vec(w0), w2.astype(BF16), vec(a0), a2.astype(BF16), g2.astype(BF16), vec(k_k), vec(k_a), vec(r_k),
            vec(gn_g), vec(gn_b), tri]
    return pl.pallas_call(
        functools.partial(_rwkv7_kernel, n_pairs=n_pairs),
        grid=(bsz, nt),
        in_specs=[rowblk(0), rowblk(1), rowblk(2),
                  pl.BlockSpec((tb, tail_w), lambda b, t: (b * nt + t, tb_col))] + [full(a) for a in args],
        out_specs=pl.BlockSpec((tb, dg), lambda b, t: (b * nt + t, 0)),
        out_shape=jax.ShapeDtypeStruct((bsz * seq, dg), BF16),
        scratch_shapes=[pltpu.VMEM((n_pairs, 2 * RW_DIM, 2 * RW_DIM), F32), pltpu.VMEM((8, dg), F32),
                        pltpu.VMEM((1, tail_w), F32)],
        compiler_params=_params(("parallel", "arbitrary")),
        name="rwkv7",
    )(proj, proj, proj, proj, *args)


def kernel(x, c, ada_w, ada_b, w_in, w_out, hg_lb_logits, hg_norm_g, s5_a_re, s5_a_im, s5_log_dt, s5_b_re, s5_b_im, s5_c_re, s5_c_im, s5_d, s5_glu_w, s5_glu_b, rw_mu, rw_w0, rw_w2, rw_a0, rw_a2, rw_g2, rw_k_k, rw_k_a, rw_r_k, rw_gn_g, rw_gn_b, ln1_g, ln1_b, ffn_w1, ffn_w3, ffn_w2, ln2_g, ln2_b):
    bsz, seq, d = x.shape
    depth = w_in.shape[0]
    n_in = w_in.shape[2]
    dg = d // N_MIXERS
    d_ff = ffn_w1.shape[2]
    alpha = (2 * depth) ** 0.25
    tokens = bsz * seq

    lb_all = jnp.cumsum(jax.nn.softmax(hg_lb_logits.astype(F32), axis=0), axis=0)
    lb_all = lb_all - lb_all[:1]

    mod = _ada_mod(c.astype(F32), ada_w, ada_b)
    mod = mod.reshape(depth, bsz, 6, 1, d)
    shift1, scale1, gate1, shift2, scale2, gate2 = [mod[:, :, i] for i in range(6)]
    zeros_vec = jnp.zeros((bsz, 1, d), F32)

    n_pad = -(-n_in // 1024) * 1024
    x2 = x.reshape(tokens, d)
    h = _modulate(x2, scale1[0], shift1[0], seq)
    for l in range(depth):
        w_in_l = jnp.pad(w_in[l].astype(BF16), ((0, 0), (0, n_pad - n_in)))
        proj, u_tm = _in_proj(h, w_in_l, seq, 4 * dg, dg, 1024, 1024)
        o_a = _hgrn2(proj, lb_all[l], hg_norm_g[l], bsz, seq, dg, 256)
        o_b = _s5(u_tm.reshape(tokens, dg), s5_a_re[l], s5_a_im[l], s5_log_dt[l], s5_b_re[l], s5_b_im[l], s5_c_re[l],
                  s5_c_im[l], s5_d[l], s5_glu_w[l], s5_glu_b[l], bsz, 64)
        o_b = o_b.reshape(seq, bsz * dg)
        o_c = _stick_breaking(proj, bsz, seq, dg, 5 * dg, 256)
        o_d = _rwkv7(proj, rw_mu[l], rw_w0[l], rw_w2[l], rw_a0[l], rw_a2[l], rw_g2[l], rw_k_k[l], rw_k_a[l],
                     rw_r_k[l].reshape(-1), rw_gn_g[l], rw_gn_b[l], bsz, seq, dg, 8 * dg, 256)
        x2, h = _out_proj_norm([o_a, o_b, o_c, o_d], w_out[l].astype(BF16), x2, gate1[l],
                               ln1_g[l].astype(F32).reshape(1, d), ln1_b[l].astype(F32).reshape(1, d),
                               scale2[l], shift2[l], seq, alpha, 512)
        u = _ffn_up(h, ffn_w1[l].astype(BF16), ffn_w3[l].astype(BF16), 1024, 512)
        last = l == depth - 1
        x2, h = _ffn_down_norm(u, ffn_w2[l].astype(BF16), x2, gate2[l],
                               ln2_g[l].astype(F32).reshape(1, d), ln2_b[l].astype(F32).reshape(1, d),
                               zeros_vec if last else scale1[l + 1], zeros_vec if last else shift1[l + 1],
                               seq, alpha, 256)
    return x2.reshape(bsz, seq, d)
```

```python
import functools
import math

import jax
import jax.numpy as jnp
from jax import lax
from jax.experimental import pallas as pl
from jax.experimental.pallas import tpu as pltpu

F32 = jnp.float32
BF16 = jnp.bfloat16

N_MIXERS = 4
CHUNK = 64
SUB = 16
HG_DIM = 128
S5_BLOCK = 256
SB_DIM = 128
RW_DIM = 64
RW_W_RANK = 64
RW_A_RANK = 64
RW_G_RANK = 128
RW_GN_EPS = 64e-5
LN_EPS = 1e-5
RMS_EPS = 1e-6
LOG2E = math.log2(math.e)
LANES = 128
V7X_VMEM_LIMIT = 56 * 1024 * 1024


def _params(sem, vmem=V7X_VMEM_LIMIT):
    return pltpu.CompilerParams(dimension_semantics=sem, vmem_limit_bytes=vmem)


def _dot(a, b):
    return jnp.dot(a, b, preferred_element_type=F32)


def _dot_nt(a, b):
    return lax.dot_general(a, b, (((1,), (1,)), ((), ())), preferred_element_type=F32)


def _dot_tn(a, b):
    return lax.dot_general(a, b, (((0,), (0,)), ((), ())), preferred_element_type=F32)


def _split3(x):
    h = x.astype(BF16)
    r = x - h.astype(F32)
    m = r.astype(BF16)
    lo = (r - m.astype(F32)).astype(BF16)
    return h, m, lo


def _dot_exact_lhs(a_bf16, x):
    h, m, lo = _split3(x)
    return _dot(a_bf16, h) + _dot(a_bf16, m) + _dot(a_bf16, lo)


def _dot_exact_rhs(x, b_bf16):
    h, m, lo = _split3(x)
    return _dot(h, b_bf16) + _dot(m, b_bf16) + _dot(lo, b_bf16)


def _softplus(x):
    return jnp.maximum(x, 0.0) + jnp.log1p(jnp.exp(-jnp.abs(x)))


def _sigmoid(x):
    return jax.nn.sigmoid(x)


def _layer_norm_rows(y, g, b):
    mu = jnp.mean(y, axis=-1, keepdims=True)
    d = y - mu
    var = jnp.mean(d * d, axis=-1, keepdims=True)
    return d * lax.rsqrt(var + LN_EPS) * g + b


def _ada_kernel(c_ref, w_ref, b_ref, o_ref):
    c = c_ref[...]
    ca = (c * _sigmoid(c)).astype(BF16)
    o_ref[0] = _dot(ca, w_ref[0].astype(BF16)) + b_ref[0]


def _ada_mod(c, ada_w, ada_b):
    depth, d, n = ada_w.shape
    bsz = c.shape[0]
    tn = 1024
    return pl.pallas_call(
        _ada_kernel,
        grid=(depth, n // tn),
        in_specs=[pl.BlockSpec((bsz, d), lambda l, j: (0, 0)),
                  pl.BlockSpec((1, d, tn), lambda l, j: (l, 0, j)),
                  pl.BlockSpec((1, 1, tn), lambda l, j: (l, 0, j))],
        out_specs=pl.BlockSpec((1, bsz, tn), lambda l, j: (l, 0, j)),
        out_shape=jax.ShapeDtypeStruct((depth, bsz, n), F32),
        compiler_params=_params(("parallel", "parallel")),
        name="ada_mod",
    )(c, ada_w, ada_b.reshape(depth, 1, n))


def _modulate_kernel(x_ref, sc_ref, sh_ref, o_ref):
    o_ref[...] = (x_ref[...] * (1.0 + sc_ref[0]) + sh_ref[0]).astype(o_ref.dtype)


def _modulate(x2, scale, shift, seq):
    t, d = x2.shape
    tm = min(1024, seq)
    per = seq // tm
    vec = pl.BlockSpec((1, 1, d), lambda i: (i // per, 0, 0))
    return pl.pallas_call(
        _modulate_kernel,
        grid=(t // tm,),
        in_specs=[pl.BlockSpec((tm, d), lambda i: (i, 0)), vec, vec],
        out_specs=pl.BlockSpec((tm, d), lambda i: (i, 0)),
        out_shape=jax.ShapeDtypeStruct((t, d), BF16),
        compiler_params=_params(("parallel",)),
        name="modulate",
    )(x2, scale, shift)


def _in_proj_kernel(a_ref, w_ref, o_ref, u_ref, *, j_u, off_u):
    acc = _dot(a_ref[...], w_ref[...])
    o_ref[...] = acc

    @pl.when(pl.program_id(1) == j_u)
    def _():
        u_ref[...] = acc[:, off_u:off_u + u_ref.shape[1]]


def _in_proj(a, w, seq, col_u, dg, tm, tn):
    m, k = a.shape
    n = w.shape[1]
    tm = min(tm, seq)
    per = seq // tm
    j_u, off_u = col_u // tn, col_u % tn
    assert off_u + dg <= tn
    return pl.pallas_call(
        functools.partial(_in_proj_kernel, j_u=j_u, off_u=off_u),
        grid=(m // tm, n // tn),
        in_specs=[pl.BlockSpec((tm, k), lambda i, j: (i, 0)),
                  pl.BlockSpec((k, tn), lambda i, j: (0, j))],
        out_specs=[pl.BlockSpec((tm, tn), lambda i, j: (i, j)),
                   pl.BlockSpec((tm, dg), lambda i, j: (i % per, i // per))],
        out_shape=[jax.ShapeDtypeStruct((m, n), F32), jax.ShapeDtypeStruct((seq, (m // seq) * dg), F32)],
        compiler_params=_params(("parallel", "arbitrary")),
        name="in_proj",
    )(a, w)


def _ffn_up_kernel(a_ref, w1_ref, w3_ref, o_ref):
    a = a_ref[...]
    p1 = _dot(a, w1_ref[...])
    p3 = _dot(a, w3_ref[...])
    o_ref[...] = (p1 * _sigmoid(p1) * p3).astype(o_ref.dtype)


def _ffn_up(a, w1, w3, tm, tn):
    m, k = a.shape
    n = w1.shape[1]
    tm = min(tm, m)
    wspec = pl.BlockSpec((k, tn), lambda i, j: (0, j))
    return pl.pallas_call(
        _ffn_up_kernel,
        grid=(m // tm, n // tn),
        in_specs=[pl.BlockSpec((tm, k), lambda i, j: (i, 0)), wspec, wspec],
        out_specs=pl.BlockSpec((tm, tn), lambda i, j: (i, j)),
        out_shape=jax.ShapeDtypeStruct((m, n), BF16),
        compiler_params=_params(("parallel", "parallel")),
        name="ffn_up",
    )(a, w1, w3)


def _residual_norm(acc, x_ref, gate_ref, lng_ref, lnb_ref, sc_ref, sh_ref, xo_ref, ho_ref, alpha):
    y = alpha * x_ref[...] + gate_ref[0] * acc
    xn = _layer_norm_rows(y, lng_ref[...], lnb_ref[...])
    xo_ref[...] = xn
    ho_ref[...] = (xn * (1.0 + sc_ref[0]) + sh_ref[0]).astype(ho_ref.dtype)


def _out_proj_kernel(oa_ref, ob_ref, oc_ref, od_ref, w_ref, x_ref, gate_ref, lng_ref, lnb_ref, sc_ref, sh_ref,
                     xo_ref, ho_ref, *, alpha, dg):
    acc = _dot(oa_ref[...], w_ref[0 * dg:1 * dg, :])
    acc += _dot(ob_ref[...], w_ref[1 * dg:2 * dg, :])
    acc += _dot(oc_ref[...], w_ref[2 * dg:3 * dg, :])
    acc += _dot(od_ref[...], w_ref[3 * dg:4 * dg, :])
    _residual_norm(acc, x_ref, gate_ref, lng_ref, lnb_ref, sc_ref, sh_ref, xo_ref, ho_ref, alpha)


def _out_proj_norm(outs, w, x2, gate, lng, lnb, scale, shift, seq, alpha, tm):
    t, d = x2.shape
    dg = outs[0].shape[1]
    tm = min(tm, seq)
    per = seq // tm
    row = lambda i: (i, 0)
    vec = pl.BlockSpec((1, 1, d), lambda i: (i // per, 0, 0))
    par = pl.BlockSpec((1, d), lambda i: (0, 0))
    return pl.pallas_call(
        functools.partial(_out_proj_kernel, alpha=alpha, dg=dg),
        grid=(t // tm,),
        in_specs=[pl.BlockSpec((tm, dg), row), pl.BlockSpec((tm, dg), lambda i: (i % per, i // per)),
                  pl.BlockSpec((tm, dg), row), pl.BlockSpec((tm, dg), row)] + [
            pl.BlockSpec(w.shape, lambda i: (0, 0), pipeline_mode=pl.Buffered(1)),
            pl.BlockSpec((tm, d), row), vec, par, par, vec, vec],
        out_specs=[pl.BlockSpec((tm, d), row), pl.BlockSpec((tm, d), row)],
        out_shape=[jax.ShapeDtypeStruct((t, d), F32), jax.ShapeDtypeStruct((t, d), BF16)],
        compiler_params=_params(("parallel",)),
        name="out_proj_norm",
    )(*outs, w, x2, gate, lng, lnb, scale, shift)


def _ffn_down_kernel(u_ref, w_ref, x_ref, gate_ref, lng_ref, lnb_ref, sc_ref, sh_ref, xo_ref, ho_ref, *, alpha):
    acc = _dot(u_ref[...], w_ref[...])
    _residual_norm(acc, x_ref, gate_ref, lng_ref, lnb_ref, sc_ref, sh_ref, xo_ref, ho_ref, alpha)


def _ffn_down_norm(u, w, x2, gate, lng, lnb, scale, shift, seq, alpha, tm):
    t, d = x2.shape
    kk = u.shape[1]
    tm = min(tm, seq)
    per = seq // tm
    row = lambda i: (i, 0)
    vec = pl.BlockSpec((1, 1, d), lambda i: (i // per, 0, 0))
    par = pl.BlockSpec((1, d), lambda i: (0, 0))
    return pl.pallas_call(
        functools.partial(_ffn_down_kernel, alpha=alpha),
        grid=(t // tm,),
        in_specs=[pl.BlockSpec((tm, kk), row),
                  pl.BlockSpec((kk, d), lambda i: (0, 0), pipeline_mode=pl.Buffered(1)),
                  pl.BlockSpec((tm, d), row), vec, par, par, vec, vec],
        out_specs=[pl.BlockSpec((tm, d), row), pl.BlockSpec((tm, d), row)],
        out_shape=[jax.ShapeDtypeStruct((t, d), F32), jax.ShapeDtypeStruct((t, d), BF16)],
        compiler_params=_params(("parallel",)),
        name="ffn_down_norm",
    )(u, w, x2, gate, lng, lnb, scale, shift)


def _hgrn2_kernel(q_ref, f_ref, i_ref, g_ref, la_ref, lc_ref, oml_ref, ng_ref, tri_ref, o_ref, st_ref, *, n_heads, n_chunks):
    @pl.when(pl.program_id(1) == 0)
    def _():
        st_ref[...] = jnp.zeros_like(st_ref)

    la = la_ref[...]
    lc = lc_ref[...]
    oml = oml_ref[...]
    ng = ng_ref[...]
    tri = tri_ref[...]
    n_sub = CHUNK // SUB
    tio = lax.broadcasted_iota(jnp.int32, (n_sub, SUB, HG_DIM), 1)
    rio = lax.broadcasted_iota(jnp.int32, (CHUNK, n_heads * HG_DIM), 0)

    def chunk_body(c, carry):
        rows = pl.ds(pl.multiple_of(c * CHUNK, CHUNK), CHUNK)
        z = f_ref[rows, :]
        log_sig = jnp.minimum(z, 0.0) - jnp.log1p(jnp.exp(-jnp.abs(z)))
        y = lc + log_sig
        mx = jnp.maximum(la, y)
        log_f = mx + jnp.log1p(jnp.exp(-jnp.abs(la - y)))
        kg = oml * _sigmoid(-z)
        q = q_ref[rows, :]
        qf = q * _sigmoid(q)
        v = i_ref[rows, :]
        g = g_ref[rows, :]
        b = _dot_exact_lhs(tri, log_f) * LOG2E
        b_last = b[CHUNK - 1:CHUNK, :]
        e_last = jnp.exp2(b_last)
        k_end = (kg * jnp.exp2(b_last - b)).astype(BF16)
        q_in = (qf * jnp.exp2(b)).astype(BF16)
        vb = v.astype(BF16)
        heads = [slice(h * HG_DIM, (h + 1) * HG_DIM) for h in range(n_heads)]
        st = [st_ref[h] for h in range(n_heads)]
        o = [_dot_nt(q_in[:, hs], st_.astype(BF16)) for hs, st_ in zip(heads, st)]
        split = lambda x: [x[:, hs].reshape(n_sub, SUB, HG_DIM) for hs in heads]
        b3, q3, k3, v3 = split(b), split(qf), split(kg), split(v)
        acc = [jnp.zeros((n_sub, SUB, HG_DIM), F32) for _ in heads]
        for s in range(SUB):
            causal = tio >= s
            for h in range(n_heads):
                e = jnp.exp2(jnp.where(causal, b3[h] - b3[h][:, s:s + 1, :], -jnp.inf))
                w = jnp.sum(q3[h] * e * k3[h][:, s:s + 1, :], axis=-1, keepdims=True)
                acc[h] = acc[h] + w * v3[h][:, s:s + 1, :]
        parts = [[jnp.zeros((SUB, HG_DIM), F32)] for _ in heads]
        for i in range(1, n_sub):
            rho = b[i * SUB - 1:i * SUB, :]
            qi = (qf[i * SUB:(i + 1) * SUB, :] * jnp.exp2(b[i * SUB:(i + 1) * SUB, :] - rho)).astype(BF16)
            ki = (kg * jnp.exp2(jnp.where(rio < i * SUB, rho - b, -jnp.inf))).astype(BF16)
            a = [_dot_nt(qi[:, hs], ki[:, hs]).astype(BF16) for hs in heads]
            for h, hs in enumerate(heads):
                parts[h].append(_dot(a[h], vb[:, hs]))
        gs = g * _sigmoid(g)
        for h, hs in enumerate(heads):
            st_ref[h] = st[h] * e_last[:, hs] + _dot_tn(vb[:, hs], k_end[:, hs])
            oh = o[h] + acc[h].reshape(CHUNK, HG_DIM) + jnp.concatenate(parts[h], axis=0)
            oh = oh * lax.rsqrt(jnp.mean(oh * oh, axis=-1, keepdims=True) + RMS_EPS) * ng[:, hs]
            o_ref[rows, hs] = (oh * gs[:, hs]).astype(o_ref.dtype)
        return carry

    lax.fori_loop(0, n_chunks, chunk_body, 0)


def _hgrn2(proj, lb, norm_g, bsz, seq, dg, tb):
    n_heads = dg // HG_DIM
    tb = min(tb, seq)
    nt = seq // tb
    lbf = lb.astype(F32).reshape(1, dg)
    la = jnp.log(jnp.maximum(lbf, 0.0))
    lc = jnp.log1p(-lbf)
    oml = 1.0 - lbf
    r = jnp.arange(CHUNK)
    tri = (r[:, None] >= r[None, :]).astype(BF16)
    col = lambda j: pl.BlockSpec((tb, dg), lambda b, t, j=j: (b * nt + t, j))
    par = pl.BlockSpec((1, dg), lambda b, t: (0, 0))
    return pl.pallas_call(
        functools.partial(_hgrn2_kernel, n_heads=n_heads, n_chunks=tb // CHUNK),
        grid=(bsz, nt),
        in_specs=[col(0), col(1), col(2), col(3), par, par, par, par,
                  pl.BlockSpec((CHUNK, CHUNK), lambda b, t: (0, 0))],
        out_specs=pl.BlockSpec((tb, dg), lambda b, t: (b * nt + t, 0)),
        out_shape=jax.ShapeDtypeStruct((bsz * seq, dg), BF16),
        scratch_shapes=[pltpu.VMEM((n_heads, HG_DIM, HG_DIM), F32)],
        compiler_params=_params(("parallel", "arbitrary")),
        name="hgrn2",
    )(proj, proj, proj, proj, la, lc, oml, norm_g.astype(F32).reshape(1, dg), tri)


def _s5_kernel(u_ref, bbar_ref, lam_ref, cmat_ref, dskip_ref, gw_ref, gb_ref, o_ref, xs_ref, h_ref, us_ref, ys_ref,
               *, tb, bsz, ns, lane_chunk):
    @pl.when(pl.program_id(0) == 0)
    def _():
        h_ref[...] = jnp.zeros_like(h_ref)

    n_col = us_ref.shape[0]
    dg = n_col * LANES
    for b in range(bsz):
        for c in range(n_col):
            us_ref[c, pl.ds(b, tb, stride=bsz), :] = u_ref[:, b * dg + c * LANES:b * dg + (c + 1) * LANES]
    u = jnp.concatenate([us_ref[c] for c in range(n_col)], axis=1)
    ub = u.astype(BF16)
    n_blk = dg // S5_BLOCK
    sb = ns // n_blk
    blocks = [(slice(b * S5_BLOCK, (b + 1) * S5_BLOCK), slice(b * sb, (b + 1) * sb),
               slice(ns + b * sb, ns + (b + 1) * sb)) for b in range(n_blk)]
    for cs, re, im in blocks:
        xs_ref[:, re] = _dot(ub[:, cs], bbar_ref[cs, re])
        xs_ref[:, im] = _dot(ub[:, cs], bbar_ref[cs, im])
    for c in range(ns // lane_chunk):
        re = slice(c * lane_chunk, (c + 1) * lane_chunk)
        im = slice(ns + c * lane_chunk, ns + (c + 1) * lane_chunk)
        lr = jnp.broadcast_to(lam_ref[0:1, re], (bsz, lane_chunk))
        li = jnp.broadcast_to(lam_ref[1:2, re], (bsz, lane_chunk))

        def step(t, carry):
            hr, hi = carry
            rows = pl.ds(pl.multiple_of(t * bsz, bsz), bsz)
            nr = lr * hr - li * hi + xs_ref[rows, re]
            ni = lr * hi + li * hr + xs_ref[rows, im]
            xs_ref[rows, re] = nr
            xs_ref[rows, im] = ni
            return nr, ni

        hr, hi = lax.fori_loop(0, tb, step, (h_ref[:, re], h_ref[:, im]), unroll=4)
        h_ref[:, re] = hr
        h_ref[:, im] = hi
    y = jnp.concatenate([_dot(xs_ref[:, re].astype(BF16), cmat_ref[re, cs])
                         + _dot(xs_ref[:, im].astype(BF16), cmat_ref[im, cs]) for cs, re, im in blocks], axis=1)
    y = y + dskip_ref[...] * u
    y = 0.5 * y * (1.0 + jnp.tanh(math.sqrt(2.0 / math.pi) * (y + 0.044715 * (y * y * y))))
    gate = _sigmoid(_dot(y.astype(BF16), gw_ref[...]) + gb_ref[...])
    y = y * gate
    for c in range(n_col):
        ys_ref[c] = y[:, c * LANES:(c + 1) * LANES]
    for b in range(bsz):
        for c in range(n_col):
            o_ref[:, b * dg + c * LANES:b * dg + (c + 1) * LANES] = (
                ys_ref[c, pl.ds(b, tb, stride=bsz), :].astype(o_ref.dtype))


def _s5(u_tm, a_re, a_im, log_dt, b_re, b_im, c_re, c_im, d_skip, glu_w, glu_b, bsz, tb):
    seq = u_tm.shape[0]
    dg = u_tm.shape[1] // bsz
    groups, p = a_re.shape
    ch = dg // groups
    ns = groups * p
    dt = jnp.exp(log_dt.astype(F32))[:, None]
    ar, ai = a_re.astype(F32), a_im.astype(F32)
    mag = jnp.exp(ar * dt)
    lr, li = mag * jnp.cos(ai * dt), mag * jnp.sin(ai * dt)
    den = ar * ar + ai * ai
    zr = ((lr - 1.0) * ar + li * ai) / den
    zi = (li * ar - (lr - 1.0) * ai) / den
    br, bi = b_re.astype(F32), b_im.astype(F32)
    bbar_r = zr[..., None] * br - zi[..., None] * bi
    bbar_i = zr[..., None] * bi + zi[..., None] * br
    eye = jnp.eye(groups, dtype=F32)

    def blockdiag_in(w):
        return jnp.einsum('gpc,gh->gchp', w, eye).reshape(groups * ch, groups * p)

    def blockdiag_out(w):
        return jnp.einsum('gcp,gh->gphc', w, eye).reshape(groups * p, groups * ch)

    bbar = jnp.concatenate([blockdiag_in(bbar_r), blockdiag_in(bbar_i)], axis=1).astype(BF16)
    cmat = jnp.concatenate([blockdiag_out(c_re.astype(F32)), -blockdiag_out(c_im.astype(F32))], axis=0).astype(BF16)
    lam = jnp.stack([lr.reshape(ns), li.reshape(ns)], axis=0)
    tb = min(tb, seq)
    full = lambda a: pl.BlockSpec(a.shape, lambda t: (0,) * a.ndim)
    dsk = d_skip.astype(F32).reshape(1, dg)
    gw = glu_w.astype(BF16)
    gb = glu_b.astype(F32).reshape(1, dg)
    return pl.pallas_call(
        functools.partial(_s5_kernel, tb=tb, bsz=bsz, ns=ns, lane_chunk=512),
        grid=(seq // tb,),
        in_specs=[pl.BlockSpec((tb, bsz * dg), lambda t: (t, 0)), full(bbar), full(lam), full(cmat), full(dsk),
                  full(gw), full(gb)],
        out_specs=pl.BlockSpec((tb, bsz * dg), lambda t: (t, 0)),
        out_shape=jax.ShapeDtypeStruct((seq, bsz * dg), BF16),
        scratch_shapes=[pltpu.VMEM((tb * bsz, 2 * ns), F32), pltpu.VMEM((bsz, 2 * ns), F32),
                        pltpu.VMEM((dg // LANES, tb * bsz, LANES), F32),
                        pltpu.VMEM((dg // LANES, tb * bsz, LANES), F32)],
        compiler_params=_params(("arbitrary",)),
        name="s5",
    )(u_tm, bbar, lam, cmat, dsk, gw, gb)


def _sb_kernel(q_ref, k_ref, v_ref, tri_ref, cmr_ref, o_ref, qb_ref, kb_ref, vb_ref, z0_ref, z1_ref, pre0_ref,
               pre1_ref, tot0_ref, tot1_ref, *, tq, nq, scale):
    qb_ref[...] = (q_ref[...] * scale).astype(BF16)
    kb_ref[...] = k_ref[...].astype(BF16)
    vb_ref[...] = v_ref[...].astype(BF16)
    z_refs, pre_refs, tot_refs = (z0_ref, z1_ref), (pre0_ref, pre1_ref), (tot0_ref, tot1_ref)
    for ref in z_refs + pre_refs + tot_refs:
        ref[...] = jnp.zeros_like(ref)
    tri = tri_ref[...]
    cmr = cmr_ref[...]
    n_steps = nq * (nq + 1) // 2 + 2

    def rows(blk):
        return pl.ds(pl.multiple_of(blk * tq, tq), tq)

    def step(c, slot):
        qa, ia, qb, ib, qc, ic, carry, acc = c
        qa_c = jnp.minimum(qa, nq - 1)
        z_refs[slot][...] = _dot_nt(qb_ref[rows(qa_c), :], kb_ref[rows(jnp.maximum(qa_c - ia, 0)), :])
        below = cmr < ib * tq
        z = jnp.where(below, z_refs[1 - slot][...], -jnp.inf)
        nz = -z
        log_rest = jnp.minimum(nz, 0.0) - jnp.log2(1.0 + jnp.exp2(jnp.minimum(z, nz)))
        after = _dot(log_rest.astype(BF16), tri)
        pre_refs[1 - slot][...] = (z + log_rest) + after
        tot_refs[1 - slot][...] = jnp.broadcast_to(jnp.sum(log_rest, axis=-1, keepdims=True), (tq, SB_DIM))
        first = ic == 0
        carry = jnp.where(first, 0.0, carry)
        acc = jnp.where(first, 0.0, acc)
        w = jnp.exp2(pre_refs[slot][...] + jnp.concatenate([carry] * (tq // SB_DIM), axis=1))
        acc = acc + _dot(w.astype(BF16), vb_ref[rows(jnp.maximum(qc - ic, 0)), :])
        o_ref[rows(qc), :] = acc.astype(o_ref.dtype)
        wrap = ia >= qa
        return (jnp.where(wrap, qa + 1, qa), jnp.where(wrap, 0, ia + 1), qa_c, ia, qb, ib,
                carry + tot_refs[slot][...], acc)

    zero = jnp.int32(0)
    c = (zero, zero, zero, zero, zero, zero, jnp.zeros((tq, SB_DIM), F32), jnp.zeros((tq, SB_DIM), F32))
    first_slot = 0
    if n_steps % 2:
        c = step(c, 0)
        first_slot = 1
    lax.fori_loop(0, n_steps // 2, lambda _, c: step(step(c, first_slot), 1 - first_slot), c)


def _stick_breaking(proj, bsz, seq, dg, col0, tq):
    n_heads = dg // SB_DIM
    tq = min(tq, seq)
    nq = seq // tq
    r = jnp.arange(tq, dtype=jnp.int32)
    tri = (r[:, None] > r[None, :]).astype(BF16)
    cmr = r[None, :] - r[:, None]
    cb = col0 // SB_DIM
    seq_blk = lambda j: pl.BlockSpec((seq, SB_DIM), lambda b, h, j=j: (b, cb + j * n_heads + h))
    const = lambda a: pl.BlockSpec(a.shape, lambda b, h: (0, 0))
    return pl.pallas_call(
        functools.partial(_sb_kernel, tq=tq, nq=nq, scale=SB_DIM ** -0.5 * math.log2(math.e)),
        grid=(bsz, n_heads),
        in_specs=[seq_blk(0), seq_blk(1), seq_blk(2), const(tri), const(cmr)],
        out_specs=pl.BlockSpec((seq, SB_DIM), lambda b, h: (b, h)),
        out_shape=jax.ShapeDtypeStruct((bsz * seq, dg), BF16),
        scratch_shapes=[pltpu.VMEM((seq, SB_DIM), BF16), pltpu.VMEM((seq, SB_DIM), BF16),
                        pltpu.VMEM((seq, SB_DIM), BF16)] + [pltpu.VMEM((tq, tq), F32)] * 4
                       + [pltpu.VMEM((tq, SB_DIM), F32)] * 2,
        compiler_params=_params(("parallel", "parallel")),
        name="stick_breaking",
    )(proj, proj, proj, tri, cmr)


def _rwkv7_kernel(r_ref, k_ref, v_ref, tail_ref, mu_r_ref, mu_k_ref, mu_v_ref, mu_t_ref, w0_ref, w2_ref, a0_ref,
                  a2_ref, g2_ref, kk_ref, ka_ref, rk_ref, gng_ref, gnb_ref, tri_ref, o_ref,
                  mt_ref, prev_ref, prev_t_ref, *, n_pairs):
    @pl.when(pl.program_id(1) == 0)
    def _():
        mt_ref[...] = jnp.zeros_like(mt_ref)
        prev_ref[...] = jnp.zeros_like(prev_ref)
        prev_t_ref[...] = jnp.zeros_like(prev_t_ref)

    def shifted(p, prev_row, mu):
        rolled = pltpu.roll(p, 1, axis=0)
        first = lax.broadcasted_iota(jnp.int32, p.shape, 0) == 0
        prev = jnp.where(first, prev_row, rolled)
        return p + mu * (prev - p)

    pr, pk, pv, pt = r_ref[...], k_ref[...], v_ref[...], tail_ref[...]
    tb = pr.shape[0]
    r = shifted(pr, prev_ref[0:1, :], mu_r_ref[...])
    k = shifted(pk, prev_ref[1:2, :], mu_k_ref[...])
    v = shifted(pv, prev_ref[2:3, :], mu_v_ref[...])
    tail = shifted(pt, prev_t_ref[...], mu_t_ref[...])
    prev_ref[0:1, :] = pr[tb - 1:tb, :]
    prev_ref[1:2, :] = pk[tb - 1:tb, :]
    prev_ref[2:3, :] = pv[tb - 1:tb, :]
    prev_t_ref[...] = pt[tb - 1:tb, :]

    xw = tail[:, 0:RW_W_RANK]
    xa = tail[:, RW_W_RANK:RW_W_RANK + RW_A_RANK]
    xg = tail[:, RW_W_RANK + RW_A_RANK:]
    w_log = -_softplus(-(w0_ref[...] + _dot(jnp.tanh(xw).astype(BF16), w2_ref[...]))) - 0.5
    logw = -jnp.exp(w_log)
    a = _sigmoid(a0_ref[...] + _dot(xa.astype(BF16), a2_ref[...]))
    g = _dot(_sigmoid(xg).astype(BF16), g2_ref[...])
    pairs = [slice(p * 2 * RW_DIM, (p + 1) * 2 * RW_DIM) for p in range(n_pairs)]
    chunks = [slice(c * CHUNK, (c + 1) * CHUNK) for c in range(tb // CHUNK)]

    def head_sum(x):
        head0 = lax.broadcasted_iota(jnp.int32, x.shape, 1) < RW_DIM
        s0 = jnp.sum(jnp.where(head0, x, 0.0), axis=-1, keepdims=True)
        s1 = jnp.sum(jnp.where(head0, 0.0, x), axis=-1, keepdims=True)
        return jnp.where(head0, s0, s1)

    def head_sums(x):
        return jnp.concatenate([head_sum(x[:, ps]) for ps in pairs], axis=1)

    kk = k * kk_ref[...]
    kk = kk * lax.rsqrt(jnp.maximum(head_sums(kk * kk), 1e-24))
    k2 = k * (1.0 + (a - 1.0) * ka_ref[...])
    beta = kk * a
    lp = _dot_exact_lhs(tri_ref[...], logw)
    e_pos = jnp.exp(lp)
    e_neg = jnp.exp(-lp)
    e_end = jnp.concatenate([jnp.exp(lp[rc][CHUNK - 1:CHUNK, :] - lp[rc]) for rc in chunks], axis=0)
    r_b = r * e_pos
    a_b = kk * jnp.exp(lp - logw)
    k_b = k2 * e_neg
    b_b = beta * e_neg
    k_t = k2 * e_end
    b_t = beta * e_end
    bonus = head_sums(r * k2 * rk_ref[...]) * v

    n2 = 2 * CHUNK
    ri = lax.broadcasted_iota(jnp.int32, (n2, n2), 0)
    ci = lax.broadcasted_iota(jnp.int32, (n2, n2), 1)
    same = (ri < CHUNK) == (ci < CHUNK)
    strict = jnp.logical_and(same, ci < ri)
    incl = jnp.logical_and(same, ci <= ri)
    eye = (ri == ci).astype(F32)
    head0 = lax.broadcasted_iota(jnp.int32, (CHUNK, 2 * RW_DIM), 1) < RW_DIM

    def stack(x):
        return jnp.concatenate([jnp.where(head0, x, 0.0), jnp.where(head0, 0.0, x)], axis=0)

    def stack2(x, y):
        return jnp.concatenate([stack(x), stack(y)], axis=0).astype(BF16)

    items = [(rc, ps) for rc in chunks for ps in pairs]
    ar = [stack2(a_b[rc, ps], r_b[rc, ps]) for rc, ps in items]
    kb = [stack2(k_b[rc, ps], b_b[rc, ps]) for rc, ps in items]
    v_s = [stack(v[rc, ps]).astype(BF16) for rc, ps in items]
    kbt = [stack2(k_t[rc, ps], b_t[rc, ps]) for rc, ps in items]
    gg = [_dot_nt(x_, y_) for x_, y_ in zip(ar, kb)]
    lg = [jnp.concatenate([jnp.where(strict, g_[:n2, :n2], 0.0), jnp.where(incl, g_[n2:, :n2], 0.0)],
                          axis=0).astype(BF16) for g_ in gg]
    lgv = [_dot(x_, y_) for x_, y_ in zip(lg, v_s)]
    gb = [jnp.where(incl, g_[n2:, n2:], 0.0).astype(BF16) for g_ in gg]
    x = [jnp.where(strict, -g_[:n2, n2:], 0.0) for g_ in gg]
    t_inv = [eye + x_ for x_ in x]
    for _ in range(int(math.log2(CHUNK)) - 1):
        xb = [x_.astype(BF16) for x_ in x]
        x = [_dot(x_, x_) for x_ in xb]
        t_inv = [t_ + _dot(t_.astype(BF16), x_.astype(BF16)) for t_, x_ in zip(t_inv, x)]
    t_inv = [t_.astype(BF16) for t_ in t_inv]

    mt = [mt_ref[p] for p in range(n_pairs)]
    for c, rc in enumerate(chunks):
        sel = range(c * n_pairs, (c + 1) * n_pairs)
        am = [_dot_nt(ar[i], mt[p].astype(BF16)) for p, i in enumerate(sel)]
        u = [_dot(t_inv[i], (am[p][:n2] + lgv[i][:n2]).astype(BF16)) for p, i in enumerate(sel)]
        o_s = [am[p][n2:] + lgv[i][n2:] - _dot(gb[i], u[p].astype(BF16)) for p, i in enumerate(sel)]
        for p, i in enumerate(sel):
            ps = pairs[p]
            vu = jnp.concatenate([v_s[i], (-u[p]).astype(BF16)], axis=0)
            mt[p] = mt[p] * e_pos[rc, ps][CHUNK - 1:CHUNK, :] + _dot_tn(vu, kbt[i])
            out = o_s[p][:CHUNK, :] + o_s[p][CHUNK:, :]
            d = out - head_sum(out) * (1.0 / RW_DIM)
            var = head_sum(d * d) * (1.0 / RW_DIM)
            out = d * lax.rsqrt(var + RW_GN_EPS) * gng_ref[:, ps] + gnb_ref[:, ps]
            o_ref[rc, ps] = ((out + bonus[rc, ps]) * g[rc, ps]).astype(o_ref.dtype)
    for p in range(n_pairs):
        mt_ref[p] = mt[p]


def _rwkv7(proj, mu, w0, w2, a0, a2, g2, k_k, k_a, r_k, gn_g, gn_b, bsz, seq, dg, col0, tb):
    n_heads = dg // RW_DIM
    n_pairs = n_heads // 2
    tb = min(tb, seq)
    nt = seq // tb
    tail_w = RW_W_RANK + RW_A_RANK + RW_G_RANK
    cb = col0 // dg
    tb_col = (col0 + 3 * dg) // tail_w
    muf = mu.astype(F32)
    vec = lambda a: a.astype(F32).reshape(1, -1)
    rr = jnp.arange(tb)
    tri = jnp.logical_and(rr[:, None] >= rr[None, :], rr[:, None] // CHUNK == rr[None, :] // CHUNK).astype(BF16)
    rowblk = lambda j: pl.BlockSpec((tb, dg), lambda b, t, j=j: (b * nt + t, cb + j))
    full = lambda a: pl.BlockSpec(a.shape, lambda b, t: (0,) * a.ndim)
    args = [vec(muf[0:dg]), vec(muf[dg:2 * dg]), vec(muf[2 * dg:3 * dg]), vec(muf[3 * dg:]),
            vec(w0), w2.astype(BF16), vec(a0), a2.astype(BF16), g2.astype(BF16), vec(k_k), vec(k_a), vec(r_k),
            vec(gn_g), vec(gn_b), tri]
    return pl.pallas_call(
        functools.partial(_rwkv7_kernel, n_pairs=n_pairs),
        grid=(bsz, nt),
        in_specs=[rowblk(0), rowblk(1), rowblk(2),
                  pl.BlockSpec((tb, tail_w), lambda b, t: (b * nt + t, tb_col))] + [full(a) for a in args],
        out_specs=pl.BlockSpec((tb, dg), lambda b, t: (b * nt + t, 0)),
        out_shape=jax.ShapeDtypeStruct((bsz * seq, dg), BF16),
        scratch_shapes=[pltpu.VMEM((n_pairs, 2 * RW_DIM, 2 * RW_DIM), F32), pltpu.VMEM((8, dg), F32),
                        pltpu.VMEM((1, tail_w), F32)],
        compiler_params=_params(("parallel", "arbitrary")),
        name="rwkv7",
    )(proj, proj, proj, proj, *args)


def kernel(x, c, ada_w, ada_b, w_in, w_out, hg_lb_logits, hg_norm_g, s5_a_re, s5_a_im, s5_log_dt, s5_b_re, s5_b_im, s5_c_re, s5_c_im, s5_d, s5_glu_w, s5_glu_b, rw_mu, rw_w0, rw_w2, rw_a0, rw_a2, rw_g2, rw_k_k, rw_k_a, rw_r_k, rw_gn_g, rw_gn_b, ln1_g, ln1_b, ffn_w1, ffn_w3, ffn_w2, ln2_g, ln2_b):
    bsz, seq, d = x.shape
    depth = w_in.shape[0]
    n_in = w_in.shape[2]
    dg = d // N_MIXERS
    d_ff = ffn_w1.shape[2]
    alpha = (2 * depth) ** 0.25
    tokens = bsz * seq

    lb_all = jnp.cumsum(jax.nn.softmax(hg_lb_logits.astype(F32), axis=0), axis=0)
    lb_all = lb_all - lb_all[:1]

    mod = _ada_mod(c.astype(F32), ada_w, ada_b)
    mod = mod.reshape(depth, bsz, 6, 1, d)
    shift1, scale1, gate1, shift2, scale2, gate2 = [mod[:, :, i] for i in range(6)]
    zeros_vec = jnp.zeros((bsz, 1, d), F32)

    n_pad = -(-n_in // 1024) * 1024
    x2 = x.reshape(tokens, d)
    h = _modulate(x2, scale1[0], shift1[0], seq)
    for l in range(depth):
        w_in_l = jnp.pad(w_in[l].astype(BF16), ((0, 0), (0, n_pad - n_in)))
        proj, u_tm = _in_proj(h, w_in_l, seq, 4 * dg, dg, 1024, 1024)
        o_a = _hgrn2(proj, lb_all[l], hg_norm_g[l], bsz, seq, dg, 256)
        o_b = _s5(u_tm, s5_a_re[l], s5_a_im[l], s5_log_dt[l], s5_b_re[l], s5_b_im[l], s5_c_re[l], s5_c_im[l],
                  s5_d[l], s5_glu_w[l], s5_glu_b[l], bsz, 64)
        o_c = _stick_breaking(proj, bsz, seq, dg, 5 * dg, 256)
        o_d = _rwkv7(proj, rw_mu[l], rw_w0[l], rw_w2[l], rw_a0[l], rw_a2[l], rw_g2[l], rw_k_k[l], rw_k_a[l],
                     rw_r_k[l].reshape(-1), rw_gn_g[l], rw_gn_b[l], bsz, seq, dg, 8 * dg, 256)
        x2, h = _out_proj_norm([o_a, o_b, o_c, o_d], w_out[l].astype(BF16), x2, gate1[l],
                               ln1_g[l].astype(F32).reshape(1, d), ln1_b[l].astype(F32).reshape(1, d),
                               scale2[l], shift2[l], seq, alpha, 512)
        u = _ffn_up(h, ffn_w1[l].astype(BF16), ffn_w3[l].astype(BF16), 1024, 512)
        last = l == depth - 1
        x2, h = _ffn_down_norm(u, ffn_w2[l].astype(BF16), x2, gate2[l],
                               ln2_g[l].astype(F32).reshape(1, d), ln2_b[l].astype(F32).reshape(1, d),
                               zeros_vec if last else scale1[l + 1], zeros_vec if last else shift1[l + 1],
                               seq, alpha, 256)
    return x2.reshape(bsz, seq, d)
```

```python
import functools
import math

import jax
import jax.numpy as jnp
from jax import lax
from jax.experimental import pallas as pl
from jax.experimental.pallas import tpu as pltpu

F32 = jnp.float32
BF16 = jnp.bfloat16

N_MIXERS = 4
CHUNK = 64
SUB = 16
HG_DIM = 128
S5_BLOCK = 256
SB_DIM = 128
RW_DIM = 64
RW_W_RANK = 64
RW_A_RANK = 64
RW_G_RANK = 128
RW_GN_EPS = 64e-5
LN_EPS = 1e-5
RMS_EPS = 1e-6
LOG2E = math.log2(math.e)
LANES = 128
V7X_VMEM_LIMIT = 56 * 1024 * 1024


def _params(sem, vmem=V7X_VMEM_LIMIT):
    return pltpu.CompilerParams(dimension_semantics=sem, vmem_limit_bytes=vmem)


def _dot(a, b):
    return jnp.dot(a, b, preferred_element_type=F32)


def _dot_nt(a, b):
    return lax.dot_general(a, b, (((1,), (1,)), ((), ())), preferred_element_type=F32)


def _dot_tn(a, b):
    return lax.dot_general(a, b, (((0,), (0,)), ((), ())), preferred_element_type=F32)


def _split3(x):
    h = x.astype(BF16)
    r = x - h.astype(F32)
    m = r.astype(BF16)
    lo = (r - m.astype(F32)).astype(BF16)
    return h, m, lo


def _dot_exact_lhs(a_bf16, x):
    h, m, lo = _split3(x)
    return _dot(a_bf16, h) + _dot(a_bf16, m) + _dot(a_bf16, lo)


def _dot_exact_rhs(x, b_bf16):
    h, m, lo = _split3(x)
    return _dot(h, b_bf16) + _dot(m, b_bf16) + _dot(lo, b_bf16)


def _softplus(x):
    return jnp.maximum(x, 0.0) + jnp.log1p(jnp.exp(-jnp.abs(x)))


def _sigmoid(x):
    return jax.nn.sigmoid(x)


def _layer_norm_rows(y, g, b):
    mu = jnp.mean(y, axis=-1, keepdims=True)
    d = y - mu
    var = jnp.mean(d * d, axis=-1, keepdims=True)
    return d * lax.rsqrt(var + LN_EPS) * g + b


def _ada_kernel(c_ref, w_ref, b_ref, o_ref):
    c = c_ref[...]
    ca = (c * _sigmoid(c)).astype(BF16)
    o_ref[0] = _dot(ca, w_ref[0].astype(BF16)) + b_ref[0]


def _ada_mod(c, ada_w, ada_b):
    depth, d, n = ada_w.shape
    bsz = c.shape[0]
    tn = 1024
    return pl.pallas_call(
        _ada_kernel,
        grid=(depth, n // tn),
        in_specs=[pl.BlockSpec((bsz, d), lambda l, j: (0, 0)),
                  pl.BlockSpec((1, d, tn), lambda l, j: (l, 0, j)),
                  pl.BlockSpec((1, 1, tn), lambda l, j: (l, 0, j))],
        out_specs=pl.BlockSpec((1, bsz, tn), lambda l, j: (l, 0, j)),
        out_shape=jax.ShapeDtypeStruct((depth, bsz, n), F32),
        compiler_params=_params(("parallel", "parallel")),
        name="ada_mod",
    )(c, ada_w, ada_b.reshape(depth, 1, n))


def _modulate_kernel(x_ref, sc_ref, sh_ref, o_ref):
    o_ref[...] = (x_ref[...] * (1.0 + sc_ref[0]) + sh_ref[0]).astype(o_ref.dtype)


def _modulate(x2, scale, shift, seq):
    t, d = x2.shape
    tm = min(1024, seq)
    per = seq // tm
    vec = pl.BlockSpec((1, 1, d), lambda i: (i // per, 0, 0))
    return pl.pallas_call(
        _modulate_kernel,
        grid=(t // tm,),
        in_specs=[pl.BlockSpec((tm, d), lambda i: (i, 0)), vec, vec],
        out_specs=pl.BlockSpec((tm, d), lambda i: (i, 0)),
        out_shape=jax.ShapeDtypeStruct((t, d), BF16),
        compiler_params=_params(("parallel",)),
        name="modulate",
    )(x2, scale, shift)


def _in_proj_kernel(a_ref, w_ref, o_ref, u_ref, *, j_u, off_u):
    acc = _dot(a_ref[...], w_ref[...])
    o_ref[...] = acc

    @pl.when(pl.program_id(1) == j_u)
    def _():
        u_ref[...] = acc[:, off_u:off_u + u_ref.shape[1]]


def _in_proj(a, w, seq, col_u, dg, tm, tn):
    m, k = a.shape
    n = w.shape[1]
    tm = min(tm, seq)
    per = seq // tm
    j_u, off_u = col_u // tn, col_u % tn
    assert off_u + dg <= tn
    return pl.pallas_call(
        functools.partial(_in_proj_kernel, j_u=j_u, off_u=off_u),
        grid=(m // tm, n // tn),
        in_specs=[pl.BlockSpec((tm, k), lambda i, j: (i, 0)),
                  pl.BlockSpec((k, tn), lambda i, j: (0, j))],
        out_specs=[pl.BlockSpec((tm, tn), lambda i, j: (i, j)),
                   pl.BlockSpec((tm, dg), lambda i, j: (i % per, i // per))],
        out_shape=[jax.ShapeDtypeStruct((m, n), F32), jax.ShapeDtypeStruct((seq, (m // seq) * dg), F32)],
        compiler_params=_params(("parallel", "arbitrary")),
        name="in_proj",
    )(a, w)


def _ffn_up_kernel(a_ref, w1_ref, w3_ref, o_ref):
    a = a_ref[...]
    p1 = _dot(a, w1_ref[...])
    p3 = _dot(a, w3_ref[...])
    o_ref[...] = (p1 * _sigmoid(p1) * p3).astype(o_ref.dtype)


def _ffn_up(a, w1, w3, tm, tn):
    m, k = a.shape
    n = w1.shape[1]
    tm = min(tm, m)
    wspec = pl.BlockSpec((k, tn), lambda i, j: (0, j))
    return pl.pallas_call(
        _ffn_up_kernel,
        grid=(m // tm, n // tn),
        in_specs=[pl.BlockSpec((tm, k), lambda i, j: (i, 0)), wspec, wspec],
        out_specs=pl.BlockSpec((tm, tn), lambda i, j: (i, j)),
        out_shape=jax.ShapeDtypeStruct((m, n), BF16),
        compiler_params=_params(("parallel", "parallel")),
        name="ffn_up",
    )(a, w1, w3)


def _residual_norm(acc, x_ref, gate_ref, lng_ref, lnb_ref, sc_ref, sh_ref, xo_ref, ho_ref, alpha):
    y = alpha * x_ref[...] + gate_ref[0] * acc
    xn = _layer_norm_rows(y, lng_ref[...], lnb_ref[...])
    xo_ref[...] = xn
    ho_ref[...] = (xn * (1.0 + sc_ref[0]) + sh_ref[0]).astype(ho_ref.dtype)


def _out_proj_kernel(oa_ref, ob_ref, oc_ref, od_ref, w_ref, x_ref, gate_ref, lng_ref, lnb_ref, sc_ref, sh_ref,
                     xo_ref, ho_ref, *, alpha, dg):
    acc = _dot(oa_ref[...], w_ref[0 * dg:1 * dg, :])
    acc += _dot(ob_ref[...], w_ref[1 * dg:2 * dg, :])
    acc += _dot(oc_ref[...], w_ref[2 * dg:3 * dg, :])
    acc += _dot(od_ref[...], w_ref[3 * dg:4 * dg, :])
    _residual_norm(acc, x_ref, gate_ref, lng_ref, lnb_ref, sc_ref, sh_ref, xo_ref, ho_ref, alpha)


def _out_proj_norm(outs, w, x2, gate, lng, lnb, scale, shift, seq, alpha, tm):
    t, d = x2.shape
    dg = outs[0].shape[1]
    tm = min(tm, seq)
    per = seq // tm
    row = lambda i: (i, 0)
    vec = pl.BlockSpec((1, 1, d), lambda i: (i // per, 0, 0))
    par = pl.BlockSpec((1, d), lambda i: (0, 0))
    return pl.pallas_call(
        functools.partial(_out_proj_kernel, alpha=alpha, dg=dg),
        grid=(t // tm,),
        in_specs=[pl.BlockSpec((tm, dg), row), pl.BlockSpec((tm, dg), lambda i: (i % per, i // per)),
                  pl.BlockSpec((tm, dg), row), pl.BlockSpec((tm, dg), row)] + [
            pl.BlockSpec(w.shape, lambda i: (0, 0), pipeline_mode=pl.Buffered(1)),
            pl.BlockSpec((tm, d), row), vec, par, par, vec, vec],
        out_specs=[pl.BlockSpec((tm, d), row), pl.BlockSpec((tm, d), row)],
        out_shape=[jax.ShapeDtypeStruct((t, d), F32), jax.ShapeDtypeStruct((t, d), BF16)],
        compiler_params=_params(("parallel",)),
        name="out_proj_norm",
    )(*outs, w, x2, gate, lng, lnb, scale, shift)


def _ffn_down_kernel(u_ref, w_ref, x_ref, gate_ref, lng_ref, lnb_ref, sc_ref, sh_ref, xo_ref, ho_ref, *, alpha):
    acc = _dot(u_ref[...], w_ref[...])
    _residual_norm(acc, x_ref, gate_ref, lng_ref, lnb_ref, sc_ref, sh_ref, xo_ref, ho_ref, alpha)


def _ffn_down_norm(u, w, x2, gate, lng, lnb, scale, shift, seq, alpha, tm):
    t, d = x2.shape
    kk = u.shape[1]
    tm = min(tm, seq)
    per = seq // tm
    row = lambda i: (i, 0)
    vec = pl.BlockSpec((1, 1, d), lambda i: (i // per, 0, 0))
    par = pl.BlockSpec((1, d), lambda i: (0, 0))
    return pl.pallas_call(
        functools.partial(_ffn_down_kernel, alpha=alpha),
        grid=(t // tm,),
        in_specs=[pl.BlockSpec((tm, kk), row),
                  pl.BlockSpec((kk, d), lambda i: (0, 0), pipeline_mode=pl.Buffered(1)),
                  pl.BlockSpec((tm, d), row), vec, par, par, vec, vec],
        out_specs=[pl.BlockSpec((tm, d), row), pl.BlockSpec((tm, d), row)],
        out_shape=[jax.ShapeDtypeStruct((t, d), F32), jax.ShapeDtypeStruct((t, d), BF16)],
        compiler_params=_params(("parallel",)),
        name="ffn_down_norm",
    )(u, w, x2, gate, lng, lnb, scale, shift)


def _hgrn2_kernel(q_ref, f_ref, i_ref, g_ref, la_ref, lc_ref, oml_ref, ng_ref, tri_ref, o_ref, st_ref, *, n_heads, n_chunks):
    @pl.when(pl.program_id(1) == 0)
    def _():
        st_ref[...] = jnp.zeros_like(st_ref)

    la = la_ref[...]
    lc = lc_ref[...]
    oml = oml_ref[...]
    ng = ng_ref[...]
    tri = tri_ref[...]
    n_sub = CHUNK // SUB
    tio = lax.broadcasted_iota(jnp.int32, (n_sub, SUB, HG_DIM), 1)
    rio = lax.broadcasted_iota(jnp.int32, (CHUNK, n_heads * HG_DIM), 0)

    def chunk_body(c, carry):
        rows = pl.ds(pl.multiple_of(c * CHUNK, CHUNK), CHUNK)
        z = f_ref[rows, :]
        log_sig = jnp.minimum(z, 0.0) - jnp.log1p(jnp.exp(-jnp.abs(z)))
        y = lc + log_sig
        mx = jnp.maximum(la, y)
        log_f = mx + jnp.log1p(jnp.exp(-jnp.abs(la - y)))
        kg = oml * _sigmoid(-z)
        q = q_ref[rows, :]
        qf = q * _sigmoid(q)
        v = i_ref[rows, :]
        g = g_ref[rows, :]
        b = _dot_exact_lhs(tri, log_f) * LOG2E
        b_last = b[CHUNK - 1:CHUNK, :]
        e_last = jnp.exp2(b_last)
        k_end = (kg * jnp.exp2(b_last - b)).astype(BF16)
        q_in = (qf * jnp.exp2(b)).astype(BF16)
        vb = v.astype(BF16)
        heads = [slice(h * HG_DIM, (h + 1) * HG_DIM) for h in range(n_heads)]
        st = [st_ref[h] for h in range(n_heads)]
        o = [_dot_nt(q_in[:, hs], st_.astype(BF16)) for hs, st_ in zip(heads, st)]
        split = lambda x: [x[:, hs].reshape(n_sub, SUB, HG_DIM) for hs in heads]
        b3, q3, k3, v3 = split(b), split(qf), split(kg), split(v)
        acc = [jnp.zeros((n_sub, SUB, HG_DIM), F32) for _ in heads]
        for s in range(SUB):
            causal = tio >= s
            for h in range(n_heads):
                e = jnp.exp2(jnp.where(causal, b3[h] - b3[h][:, s:s + 1, :], -jnp.inf))
                w = jnp.sum(q3[h] * e * k3[h][:, s:s + 1, :], axis=-1, keepdims=True)
                acc[h] = acc[h] + w * v3[h][:, s:s + 1, :]
        parts = [[jnp.zeros((SUB, HG_DIM), F32)] for _ in heads]
        for i in range(1, n_sub):
            rho = b[i * SUB - 1:i * SUB, :]
            qi = (qf[i * SUB:(i + 1) * SUB, :] * jnp.exp2(b[i * SUB:(i + 1) * SUB, :] - rho)).astype(BF16)
            ki = (kg * jnp.exp2(jnp.where(rio < i * SUB, rho - b, -jnp.inf))).astype(BF16)
            a = [_dot_nt(qi[:, hs], ki[:, hs]).astype(BF16) for hs in heads]
            for h, hs in enumerate(heads):
                parts[h].append(_dot(a[h], vb[:, hs]))
        gs = g * _sigmoid(g)
        for h, hs in enumerate(heads):
            st_ref[h] = st[h] * e_last[:, hs] + _dot_tn(vb[:, hs], k_end[:, hs])
            oh = o[h] + acc[h].reshape(CHUNK, HG_DIM) + jnp.concatenate(parts[h], axis=0)
            oh = oh * lax.rsqrt(jnp.mean(oh * oh, axis=-1, keepdims=True) + RMS_EPS) * ng[:, hs]
            o_ref[rows, hs] = (oh * gs[:, hs]).astype(o_ref.dtype)
        return carry

    lax.fori_loop(0, n_chunks, chunk_body, 0)


def _hgrn2(proj, lb, norm_g, bsz, seq, dg, tb):
    n_heads = dg // HG_DIM
    tb = min(tb, seq)
    nt = seq // tb
    lbf = lb.astype(F32).reshape(1, dg)
    la = jnp.log(jnp.maximum(lbf, 0.0))
    lc = jnp.log1p(-lbf)
    oml = 1.0 - lbf
    r = jnp.arange(CHUNK)
    tri = (r[:, None] >= r[None, :]).astype(BF16)
    col = lambda j: pl.BlockSpec((tb, dg), lambda b, t, j=j: (b * nt + t, j))
    par = pl.BlockSpec((1, dg), lambda b, t: (0, 0))
    return pl.pallas_call(
        functools.partial(_hgrn2_kernel, n_heads=n_heads, n_chunks=tb // CHUNK),
        grid=(bsz, nt),
        in_specs=[col(0), col(1), col(2), col(3), par, par, par, par,
                  pl.BlockSpec((CHUNK, CHUNK), lambda b, t: (0, 0))],
        out_specs=pl.BlockSpec((tb, dg), lambda b, t: (b * nt + t, 0)),
        out_shape=jax.ShapeDtypeStruct((bsz * seq, dg), BF16),
        scratch_shapes=[pltpu.VMEM((n_heads, HG_DIM, HG_DIM), F32)],
        compiler_params=_params(("parallel", "arbitrary")),
        name="hgrn2",
    )(proj, proj, proj, proj, la, lc, oml, norm_g.astype(F32).reshape(1, dg), tri)


def _s5_kernel(u_ref, bbar_ref, lam_ref, cmat_ref, dskip_ref, gw_ref, gb_ref, o_ref, xs_ref, h_ref, us_ref, ys_ref,
               *, tb, bsz, ns, lane_chunk):
    @pl.when(pl.program_id(0) == 0)
    def _():
        h_ref[...] = jnp.zeros_like(h_ref)

    n_col = us_ref.shape[0]
    dg = n_col * LANES
    for b in range(bsz):
        for c in range(n_col):
            us_ref[c, pl.ds(b, tb, stride=bsz), :] = u_ref[:, b * dg + c * LANES:b * dg + (c + 1) * LANES]
    u = jnp.concatenate([us_ref[c] for c in range(n_col)], axis=1)
    ub = u.astype(BF16)
    n_blk = dg // S5_BLOCK
    sb = ns // n_blk
    blocks = [(slice(b * S5_BLOCK, (b + 1) * S5_BLOCK), slice(b * sb, (b + 1) * sb),
               slice(ns + b * sb, ns + (b + 1) * sb)) for b in range(n_blk)]
    for cs, re, im in blocks:
        xs_ref[:, re] = _dot(ub[:, cs], bbar_ref[cs, re])
        xs_ref[:, im] = _dot(ub[:, cs], bbar_ref[cs, im])
    for c in range(ns // lane_chunk):
        re = slice(c * lane_chunk, (c + 1) * lane_chunk)
        im = slice(ns + c * lane_chunk, ns + (c + 1) * lane_chunk)
        lr = jnp.broadcast_to(lam_ref[0:1, re], (bsz, lane_chunk))
        li = jnp.broadcast_to(lam_ref[1:2, re], (bsz, lane_chunk))

        def step(t, carry):
            hr, hi = carry
            rows = pl.ds(pl.multiple_of(t * bsz, bsz), bsz)
            nr = lr * hr - li * hi + xs_ref[rows, re]
            ni = lr * hi + li * hr + xs_ref[rows, im]
            xs_ref[rows, re] = nr
            xs_ref[rows, im] = ni
            return nr, ni

        hr, hi = lax.fori_loop(0, tb, step, (h_ref[:, re], h_ref[:, im]), unroll=4)
        h_ref[:, re] = hr
        h_ref[:, im] = hi
    y = jnp.concatenate([_dot(xs_ref[:, re].astype(BF16), cmat_ref[re, cs])
                         + _dot(xs_ref[:, im].astype(BF16), cmat_ref[im, cs]) for cs, re, im in blocks], axis=1)
    y = y + dskip_ref[...] * u
    y = 0.5 * y * (1.0 + jnp.tanh(math.sqrt(2.0 / math.pi) * (y + 0.044715 * (y * y * y))))
    gate = _sigmoid(_dot(y.astype(BF16), gw_ref[...]) + gb_ref[...])
    y = y * gate
    for c in range(n_col):
        ys_ref[c] = y[:, c * LANES:(c + 1) * LANES]
    for b in range(bsz):
        for c in range(n_col):
            o_ref[:, b * dg + c * LANES:b * dg + (c + 1) * LANES] = (
                ys_ref[c, pl.ds(b, tb, stride=bsz), :].astype(o_ref.dtype))


def _s5(u_tm, a_re, a_im, log_dt, b_re, b_im, c_re, c_im, d_skip, glu_w, glu_b, bsz, tb):
    seq = u_tm.shape[0]
    dg = u_tm.shape[1] // bsz
    groups, p = a_re.shape
    ch = dg // groups
    ns = groups * p
    dt = jnp.exp(log_dt.astype(F32))[:, None]
    ar, ai = a_re.astype(F32), a_im.astype(F32)
    mag = jnp.exp(ar * dt)
    lr, li = mag * jnp.cos(ai * dt), mag * jnp.sin(ai * dt)
    den = ar * ar + ai * ai
    zr = ((lr - 1.0) * ar + li * ai) / den
    zi = (li * ar - (lr - 1.0) * ai) / den
    br, bi = b_re.astype(F32), b_im.astype(F32)
    bbar_r = zr[..., None] * br - zi[..., None] * bi
    bbar_i = zr[..., None] * bi + zi[..., None] * br
    eye = jnp.eye(groups, dtype=F32)

    def blockdiag_in(w):
        return jnp.einsum('gpc,gh->gchp', w, eye).reshape(groups * ch, groups * p)

    def blockdiag_out(w):
        return jnp.einsum('gcp,gh->gphc', w, eye).reshape(groups * p, groups * ch)

    bbar = jnp.concatenate([blockdiag_in(bbar_r), blockdiag_in(bbar_i)], axis=1).astype(BF16)
    cmat = jnp.concatenate([blockdiag_out(c_re.astype(F32)), -blockdiag_out(c_im.astype(F32))], axis=0).astype(BF16)
    lam = jnp.stack([lr.reshape(ns), li.reshape(ns)], axis=0)
    tb = min(tb, seq)
    full = lambda a: pl.BlockSpec(a.shape, lambda t: (0,) * a.ndim)
    dsk = d_skip.astype(F32).reshape(1, dg)
    gw = glu_w.astype(BF16)
    gb = glu_b.astype(F32).reshape(1, dg)
    return pl.pallas_call(
        functools.partial(_s5_kernel, tb=tb, bsz=bsz, ns=ns, lane_chunk=512),
        grid=(seq // tb,),
        in_specs=[pl.BlockSpec((tb, bsz * dg), lambda t: (t, 0)), full(bbar), full(lam), full(cmat), full(dsk),
                  full(gw), full(gb)],
        out_specs=pl.BlockSpec((tb, bsz * dg), lambda t: (t, 0)),
        out_shape=jax.ShapeDtypeStruct((seq, bsz * dg), BF16),
        scratch_shapes=[pltpu.VMEM((tb * bsz, 2 * ns), F32), pltpu.VMEM((bsz, 2 * ns), F32),
                        pltpu.VMEM((dg // LANES, tb * bsz, LANES), F32),
                        pltpu.VMEM((dg // LANES, tb * bsz, LANES), F32)],
        compiler_params=_params(("arbitrary",)),
        name="s5",
    )(u_tm, bbar, lam, cmat, dsk, gw, gb)


def _sb_kernel(q_ref, k_ref, v_ref, tri_ref, cmr_ref, o_ref, qb_ref, kb_ref, vb_ref, z0_ref, z1_ref, pre0_ref,
               pre1_ref, tot0_ref, tot1_ref, *, tq, nq, scale):
    qb_ref[...] = (q_ref[...] * scale).astype(BF16)
    kb_ref[...] = k_ref[...].astype(BF16)
    vb_ref[...] = v_ref[...].astype(BF16)
    z_refs, pre_refs, tot_refs = (z0_ref, z1_ref), (pre0_ref, pre1_ref), (tot0_ref, tot1_ref)
    for ref in z_refs + pre_refs + tot_refs:
        ref[...] = jnp.zeros_like(ref)
    tri = tri_ref[...]
    cmr = cmr_ref[...]
    n_steps = nq * (nq + 1) // 2 + 2

    def rows(blk):
        return pl.ds(pl.multiple_of(blk * tq, tq), tq)

    def step(c, slot):
        qa, ia, qb, ib, qc, ic, carry, acc = c
        qa_c = jnp.minimum(qa, nq - 1)
        z_refs[slot][...] = _dot_nt(qb_ref[rows(qa_c), :], kb_ref[rows(jnp.maximum(qa_c - ia, 0)), :])
        below = cmr < ib * tq
        z = jnp.where(below, z_refs[1 - slot][...], -jnp.inf)
        nz = -z
        log_rest = jnp.minimum(nz, 0.0) - jnp.log2(1.0 + jnp.exp2(jnp.minimum(z, nz)))
        after = _dot(log_rest.astype(BF16), tri)
        pre_refs[1 - slot][...] = (z + log_rest) + after
        tot_refs[1 - slot][...] = jnp.broadcast_to(jnp.sum(log_rest, axis=-1, keepdims=True), (tq, SB_DIM))
        first = ic == 0
        carry = jnp.where(first, 0.0, carry)
        acc = jnp.where(first, 0.0, acc)
        w = jnp.exp2(pre_refs[slot][...] + jnp.concatenate([carry] * (tq // SB_DIM), axis=1))
        acc = acc + _dot(w.astype(BF16), vb_ref[rows(jnp.maximum(qc - ic, 0)), :])
        o_ref[rows(qc), :] = acc.astype(o_ref.dtype)
        wrap = ia >= qa
        return (jnp.where(wrap, qa + 1, qa), jnp.where(wrap, 0, ia + 1), qa_c, ia, qb, ib,
                carry + tot_refs[slot][...], acc)

    zero = jnp.int32(0)
    c = (zero, zero, zero, zero, zero, zero, jnp.zeros((tq, SB_DIM), F32), jnp.zeros((tq, SB_DIM), F32))
    first_slot = 0
    if n_steps % 2:
        c = step(c, 0)
        first_slot = 1
    lax.fori_loop(0, n_steps // 2, lambda _, c: step(step(c, first_slot), 1 - first_slot), c)


def _stick_breaking(proj, bsz, seq, dg, col0, tq):
    n_heads = dg // SB_DIM
    tq = min(tq, seq)
    nq = seq // tq
    r = jnp.arange(tq, dtype=jnp.int32)
    tri = (r[:, None] > r[None, :]).astype(BF16)
    cmr = r[None, :] - r[:, None]
    cb = col0 // SB_DIM
    seq_blk = lambda j: pl.BlockSpec((seq, SB_DIM), lambda b, h, j=j: (b, cb + j * n_heads + h))
    const = lambda a: pl.BlockSpec(a.shape, lambda b, h: (0, 0))
    return pl.pallas_call(
        functools.partial(_sb_kernel, tq=tq, nq=nq, scale=SB_DIM ** -0.5 * math.log2(math.e)),
        grid=(bsz, n_heads),
        in_specs=[seq_blk(0), seq_blk(1), seq_blk(2), const(tri), const(cmr)],
        out_specs=pl.BlockSpec((seq, SB_DIM), lambda b, h: (b, h)),
        out_shape=jax.ShapeDtypeStruct((bsz * seq, dg), BF16),
        scratch_shapes=[pltpu.VMEM((seq, SB_DIM), BF16), pltpu.VMEM((seq, SB_DIM), BF16),
                        pltpu.VMEM((seq, SB_DIM), BF16)] + [pltpu.VMEM((tq, tq), F32)] * 4
                       + [pltpu.VMEM((tq, SB_DIM), F32)] * 2,
        compiler_params=_params(("parallel", "parallel")),
        name="stick_breaking",
    )(proj, proj, proj, tri, cmr)


def _rwkv7_kernel(r_ref, k_ref, v_ref, tail_ref, mu_r_ref, mu_k_ref, mu_v_ref, mu_t_ref, w0_ref, w2_ref, a0_ref,
                  a2_ref, g2_ref, kk_ref, ka_ref, rk_ref, gng_ref, gnb_ref, tri_ref, o_ref,
                  mt_ref, prev_ref, prev_t_ref, *, n_pairs):
    @pl.when(pl.program_id(1) == 0)
    def _():
        mt_ref[...] = jnp.zeros_like(mt_ref)
        prev_ref[...] = jnp.zeros_like(prev_ref)
        prev_t_ref[...] = jnp.zeros_like(prev_t_ref)

    def shifted(p, prev_row, mu):
        rolled = pltpu.roll(p, 1, axis=0)
        first = lax.broadcasted_iota(jnp.int32, p.shape, 0) == 0
        prev = jnp.where(first, prev_row, rolled)
        return p + mu * (prev - p)

    pr, pk, pv, pt = r_ref[...], k_ref[...], v_ref[...], tail_ref[...]
    tb = pr.shape[0]
    r = shifted(pr, prev_ref[0:1, :], mu_r_ref[...])
    k = shifted(pk, prev_ref[1:2, :], mu_k_ref[...])
    v = shifted(pv, prev_ref[2:3, :], mu_v_ref[...])
    tail = shifted(pt, prev_t_ref[...], mu_t_ref[...])
    prev_ref[0:1, :] = pr[tb - 1:tb, :]
    prev_ref[1:2, :] = pk[tb - 1:tb, :]
    prev_ref[2:3, :] = pv[tb - 1:tb, :]
    prev_t_ref[...] = pt[tb - 1:tb, :]

    xw = tail[:, 0:RW_W_RANK]
    xa = tail[:, RW_W_RANK:RW_W_RANK + RW_A_RANK]
    xg = tail[:, RW_W_RANK + RW_A_RANK:]
    w_log = -_softplus(-(w0_ref[...] + _dot(jnp.tanh(xw).astype(BF16), w2_ref[...]))) - 0.5
    logw = -jnp.exp(w_log)
    a = _sigmoid(a0_ref[...] + _dot(xa.astype(BF16), a2_ref[...]))
    g = _dot(_sigmoid(xg).astype(BF16), g2_ref[...])
    pairs = [slice(p * 2 * RW_DIM, (p + 1) * 2 * RW_DIM) for p in range(n_pairs)]
    chunks = [slice(c * CHUNK, (c + 1) * CHUNK) for c in range(tb // CHUNK)]

    def head_sum(x):
        head0 = lax.broadcasted_iota(jnp.int32, x.shape, 1) < RW_DIM
        s0 = jnp.sum(jnp.where(head0, x, 0.0), axis=-1, keepdims=True)
        s1 = jnp.sum(jnp.where(head0, 0.0, x), axis=-1, keepdims=True)
        return jnp.where(head0, s0, s1)

    def head_sums(x):
        return jnp.concatenate([head_sum(x[:, ps]) for ps in pairs], axis=1)

    kk = k * kk_ref[...]
    kk = kk * lax.rsqrt(jnp.maximum(head_sums(kk * kk), 1e-24))
    k2 = k * (1.0 + (a - 1.0) * ka_ref[...])
    beta = kk * a
    lp = _dot_exact_lhs(tri_ref[...], logw)
    e_pos = jnp.exp(lp)
    e_neg = jnp.exp(-lp)
    e_end = jnp.concatenate([jnp.exp(lp[rc][CHUNK - 1:CHUNK, :] - lp[rc]) for rc in chunks], axis=0)
    r_b = r * e_pos
    a_b = kk * jnp.exp(lp - logw)
    k_b = k2 * e_neg
    b_b = beta * e_neg
    k_t = k2 * e_end
    b_t = beta * e_end
    bonus = head_sums(r * k2 * rk_ref[...]) * v

    n2 = 2 * CHUNK
    ri = lax.broadcasted_iota(jnp.int32, (n2, n2), 0)
    ci = lax.broadcasted_iota(jnp.int32, (n2, n2), 1)
    same = (ri < CHUNK) == (ci < CHUNK)
    strict = jnp.logical_and(same, ci < ri)
    incl = jnp.logical_and(same, ci <= ri)
    eye = (ri == ci).astype(F32)
    head0 = lax.broadcasted_iota(jnp.int32, (CHUNK, 2 * RW_DIM), 1) < RW_DIM

    def stack(x):
        return jnp.concatenate([jnp.where(head0, x, 0.0), jnp.where(head0, 0.0, x)], axis=0)

    def stack2(x, y):
        return jnp.concatenate([stack(x), stack(y)], axis=0).astype(BF16)

    items = [(rc, ps) for rc in chunks for ps in pairs]
    ar = [stack2(a_b[rc, ps], r_b[rc, ps]) for rc, ps in items]
    kb = [stack2(k_b[rc, ps], b_b[rc, ps]) for rc, ps in items]
    v_s = [stack(v[rc, ps]).astype(BF16) for rc, ps in items]
    kbt = [stack2(k_t[rc, ps], b_t[rc, ps]) for rc, ps in items]
    gg = [_dot_nt(x_, y_) for x_, y_ in zip(ar, kb)]
    lg = [jnp.concatenate([jnp.where(strict, g_[:n2, :n2], 0.0), jnp.where(incl, g_[n2:, :n2], 0.0)],
                          axis=0).astype(BF16) for g_ in gg]
    lgv = [_dot(x_, y_) for x_, y_ in zip(lg, v_s)]
    gb = [jnp.where(incl, g_[n2:, n2:], 0.0).astype(BF16) for g_ in gg]
    x = [jnp.where(strict, -g_[:n2, n2:], 0.0) for g_ in gg]
    t_inv = [eye + x_ for x_ in x]
    for _ in range(int(math.log2(CHUNK)) - 1):
        xb = [x_.astype(BF16) for x_ in x]
        x = [_dot(x_, x_) for x_ in xb]
        t_inv = [t_ + _dot(t_.astype(BF16), x_.astype(BF16)) for t_, x_ in zip(t_inv, x)]
    t_inv = [t_.astype(BF16) for t_ in t_inv]

    mt = [mt_ref[p] for p in range(n_pairs)]
    for c, rc in enumerate(chunks):
        sel = range(c * n_pairs, (c + 1) * n_pairs)
        am = [_dot_nt(ar[i], mt[p].astype(BF16)) for p, i in enumerate(sel)]
        u = [_dot(t_inv[i], (am[p][:n2] + lgv[i][:n2]).astype(BF16)) for p, i in enumerate(sel)]
        o_s = [am[p][n2:] + lgv[i][n2:] - _dot(gb[i], u[p].astype(BF16)) for p, i in enumerate(sel)]
        for p, i in enumerate(sel):
            ps = pairs[p]
            vu = jnp.concatenate([v_s[i], (-u[p]).astype(BF16)], axis=0)
            mt[p] = mt[p] * e_pos[rc, ps][CHUNK - 1:CHUNK, :] + _dot_tn(vu, kbt[i])
            out = o_s[p][:CHUNK, :] + o_s[p][CHUNK:, :]
            d = out - head_sum(out) * (1.0 / RW_DIM)
            var = head_sum(d * d) * (1.0 / RW_DIM)
            out = d * lax.rsqrt(var + RW_GN_EPS) * gng_ref[:, ps] + gnb_ref[:, ps]
            o_ref[rc, ps] = ((out + bonus[rc, ps]) * g[rc, ps]).astype(o_ref.dtype)
    for p in range(n_pairs):
        mt_ref[p] = mt[p]


def _rwkv7(proj, mu, w0, w2, a0, a2, g2, k_k, k_a, r_k, gn_g, gn_b, bsz, seq, dg, col0, tb):
    n_heads = dg // RW_DIM
    n_pairs = n_heads // 2
    tb = min(tb, seq)
    nt = seq // tb
    tail_w = RW_W_RANK + RW_A_RANK + RW_G_RANK
    cb = col0 // dg
    tb_col = (col0 + 3 * dg) // tail_w
    muf = mu.astype(F32)
    vec = lambda a: a.astype(F32).reshape(1, -1)
    rr = jnp.arange(tb)
    tri = jnp.logical_and(rr[:, None] >= rr[None, :], rr[:, None] // CHUNK == rr[None, :] // CHUNK).astype(BF16)
    rowblk = lambda j: pl.BlockSpec((tb, dg), lambda b, t, j=j: (b * nt + t, cb + j))
    full = lambda a: pl.BlockSpec(a.shape, lambda b, t: (0,) * a.ndim)
    args = [vec(muf[0:dg]), vec(muf[dg:2 * dg]), vec(muf[2 * dg:3 * dg]), vec(muf[3 * dg:]),
            vec(w0), w2.astype(BF16), vec(a0), a2.astype(BF16), g2.astype(BF16), vec(k_k), vec(k_a), vec(r_k),
            vec(gn_g), vec(gn_b), tri]
    return pl.pallas_call(
        functools.partial(_rwkv7_kernel, n_pairs=n_pairs),
        grid=(bsz, nt),
        in_specs=[rowblk(0), rowblk(1), rowblk(2),
                  pl.BlockSpec((tb, tail_w), lambda b, t: (b * nt + t, tb_col))] + [full(a) for a in args],
        out_specs=pl.BlockSpec((tb, dg), lambda b, t: (b * nt + t, 0)),
        out_shape=jax.ShapeDtypeStruct((bsz * seq, dg), BF16),
        scratch_shapes=[pltpu.VMEM((n_pairs, 2 * RW_DIM, 2 * RW_DIM), F32), pltpu.VMEM((8, dg), F32),
                        pltpu.VMEM((1, tail_w), F32)],
        compiler_params=_params(("parallel", "arbitrary")),
        name="rwkv7",
    )(proj, proj, proj, proj, *args)


def kernel(x, c, ada_w, ada_b, w_in, w_out, hg_lb_logits, hg_norm_g, s5_a_re, s5_a_im, s5_log_dt, s5_b_re, s5_b_im, s5_c_re, s5_c_im, s5_d, s5_glu_w, s5_glu_b, rw_mu, rw_w0, rw_w2, rw_a0, rw_a2, rw_g2, rw_k_k, rw_k_a, rw_r_k, rw_gn_g, rw_gn_b, ln1_g, ln1_b, ffn_w1, ffn_w3, ffn_w2, ln2_g, ln2_b):
    bsz, seq, d = x.shape
    depth = w_in.shape[0]
    n_in = w_in.shape[2]
    dg = d // N_MIXERS
    d_ff = ffn_w1.shape[2]
    alpha = (2 * depth) ** 0.25
    tokens = bsz * seq

    lb_all = jnp.cumsum(jax.nn.softmax(hg_lb_logits.astype(F32), axis=0), axis=0)
    lb_all = lb_all - lb_all[:1]

    mod = _ada_mod(c.astype(F32), ada_w, ada_b)
    mod = mod.reshape(depth, bsz, 6, 1, d)
    shift1, scale1, gate1, shift2, scale2, gate2 = [mod[:, :, i] for i in range(6)]
    zeros_vec = jnp.zeros((bsz, 1, d), F32)

    n_pad = n_in
    x2 = x.reshape(tokens, d)
    h = _modulate(x2, scale1[0], shift1[0], seq)
    for l in range(depth):
        w_in_l = jnp.pad(w_in[l].astype(BF16), ((0, 0), (0, n_pad - n_in)))
        proj, u_tm = _in_proj(h, w_in_l, seq, 4 * dg, dg, 512, n_in // 2)
        o_a = _hgrn2(proj, lb_all[l], hg_norm_g[l], bsz, seq, dg, 256)
        o_b = _s5(u_tm, s5_a_re[l], s5_a_im[l], s5_log_dt[l], s5_b_re[l], s5_b_im[l], s5_c_re[l], s5_c_im[l],
                  s5_d[l], s5_glu_w[l], s5_glu_b[l], bsz, 64)
        o_c = _stick_breaking(proj, bsz, seq, dg, 5 * dg, 256)
        o_d = _rwkv7(proj, rw_mu[l], rw_w0[l], rw_w2[l], rw_a0[l], rw_a2[l], rw_g2[l], rw_k_k[l], rw_k_a[l],
                     rw_r_k[l].reshape(-1), rw_gn_g[l], rw_gn_b[l], bsz, seq, dg, 8 * dg, 256)
        x2, h = _out_proj_norm([o_a, o_b, o_c, o_d], w_out[l].astype(BF16), x2, gate1[l],
                               ln1_g[l].astype(F32).reshape(1, d), ln1_b[l].astype(F32).reshape(1, d),
                               scale2[l], shift2[l], seq, alpha, 512)
        u = _ffn_up(h, ffn_w1[l].astype(BF16), ffn_w3[l].astype(BF16), 1024, 512)
        last = l == depth - 1
        x2, h = _ffn_down_norm(u, ffn_w2[l].astype(BF16), x2, gate2[l],
                               ln2_g[l].astype(F32).reshape(1, d), ln2_b[l].astype(F32).reshape(1, d),
                               zeros_vec if last else scale1[l + 1], zeros_vec if last else shift1[l + 1],
                               seq, alpha, 256)
    return x2.reshape(bsz, seq, d)
```
